```python
import jax
import jax.numpy as jnp
from jax import lax
import numpy as np

D_MODEL = 1024
BATCH = 1
SEQ = 16384
DEPTH = 2
DEC_BATCH = 32
DEC_SEQ = 1
PAST_LEN = 16384
PAGE_SIZE = 128

H_A = 8
DK_A = 128
DV_A = 128
H_B = 8
D_B = 128
CONV_W = 4
GDN_CHUNK = 64
SB_Q_BLOCK = 128
SB_BIAS_INIT = -6.0
QK_A = H_A * DK_A
V_A = H_A * DV_A
W_B = H_B * D_B
CONV_DIM_A = 2 * QK_A + V_A
IN_EVEN = CONV_DIM_A + 2 * H_A + V_A + 4 * W_B
MIX_EVEN = V_A + W_B
D_INNER_C = 2 * D_MODEL
HEAD_DIM_C = 64
H_C = D_INNER_C // HEAD_DIM_C
N_GROUPS_C = 4
D_STATE_C = 128
SSD_CHUNK = 64
CONV_DIM_C = D_INNER_C + 2 * N_GROUPS_C * D_STATE_C
IN_ODD = D_INNER_C + CONV_DIM_C + H_C
N_EVEN = (DEPTH + 1) // 2
N_ODD = DEPTH // 2
RMS_EPS = 1e-6

kernel_name = 'stick_delta_ssd_hybrid_step'


def _split(a, sizes):
    idx = np.cumsum(sizes)[:-1].tolist()
    return jnp.split(a, idx, axis=-1)


def _pad_time(a, n):
    return jnp.pad(a, [(0, 0), (0, n)] + [(0, 0)] * (a.ndim - 2))


def rmsnorm(x, w):
    xf = x.astype(jnp.float32)
    y = xf * lax.rsqrt(jnp.mean(xf * xf, axis=-1, keepdims=True) + RMS_EPS)
    return (y * w.astype(jnp.float32)).astype(x.dtype)


def l2norm(x):
    xf = x.astype(jnp.float32)
    return xf * lax.rsqrt(jnp.sum(xf * xf, axis=-1, keepdims=True) + 1e-6)


def causal_conv(x, buf, w, b=None):
    T = x.shape[1]
    xp = jnp.concatenate([buf.astype(x.dtype), x], axis=1)
    y = sum(xp[:, i:i + T] * w[i] for i in range(CONV_W))
    if b is not None:
        y = y + b
    return jax.nn.silu(y), xp[:, -(CONV_W - 1):]


def gated_delta_chunked(q, k, v, g, beta, s0):
    Bsz, T, H, DK = q.shape
    DV = v.shape[-1]
    C = GDN_CHUNK
    n = -(-T // C)
    pad = n * C - T

    def chunk(a):
        a = _pad_time(a.astype(jnp.float32), pad)
        return jnp.moveaxis(a.reshape((Bsz, n, C) + a.shape[2:]), 3, 1)

    q = chunk(q) * (DK ** -0.5)
    k, v, g, beta = chunk(k), chunk(v), chunk(g), chunk(beta)
    gc = jnp.cumsum(g, axis=-1)
    tril_incl = jnp.tril(jnp.ones((C, C), bool))
    tril_strict = jnp.tril(jnp.ones((C, C), bool), -1)
    diff = gc[..., :, None] - gc[..., None, :]
    decay = jnp.where(tril_incl, jnp.exp(jnp.where(tril_incl, diff, 0.0)), 0.0)
    kk = jnp.einsum('bhnid,bhnjd->bhnij', k, k)
    lmat = jnp.where(tril_strict, beta[..., None] * kk * decay, 0.0)
    rhs = jnp.concatenate([v * beta[..., None], k * (beta * jnp.exp(gc))[..., None]], axis=-1)
    sol = lax.linalg.triangular_solve(jnp.eye(C, dtype=jnp.float32) + lmat, rhs,
                                      left_side=True, lower=True, unit_diagonal=True)
    u, w = sol[..., :DV], sol[..., DV:]
    qk = jnp.where(tril_incl, jnp.einsum('bhnid,bhnjd->bhnij', q, k) * decay, 0.0)
    qg = q * jnp.exp(gc)[..., None]
    kdec = k * jnp.exp(gc[..., -1:] - gc)[..., None]
    glast = gc[..., -1]

    def step(S, xs):
        u_c, w_c, qk_c, qg_c, kdec_c, gl_c = xs
        v_new = u_c - jnp.einsum('bhcd,bhde->bhce', w_c, S)
        o = jnp.einsum('bhcd,bhde->bhce', qg_c, S) + jnp.einsum('bhij,bhje->bhie', qk_c, v_new)
        S = S * jnp.exp(gl_c)[..., None, None] + jnp.einsum('bhcd,bhce->bhde', kdec_c, v_new)
        return S, o

    xs = tuple(jnp.moveaxis(a, 2, 0) for a in (u, w, qk, qg, kdec, glast))
    s_final, o = lax.scan(step, s0.astype(jnp.float32), xs)
    o = jnp.transpose(o, (1, 0, 3, 2, 4)).reshape(Bsz, n * C, H, DV)[:, :T]
    return o, s_final


def stick_breaking(q, k, v, bias, q_pos, k_pos):
    Bsz, Tq, H, D = q.shape
    qb = min(SB_Q_BLOCK, Tq)
    n = -(-Tq // qb)
    pad = n * qb - Tq
    qblocks = jnp.moveaxis(_pad_time(q, pad).reshape(Bsz, n, qb, H, D), 1, 0)
    pblocks = jnp.pad(q_pos, (0, pad)).reshape(n, qb)
    scale = D ** -0.5
    b = bias.astype(jnp.float32)[None, :, None, None]

    def block(args):
        qblk, pblk = args
        z = jnp.einsum('bqhd,bkhd->bhqk', qblk, k).astype(jnp.float32) * scale + b
        mask = k_pos[None, :] < pblk[:, None]
        sp = jnp.where(mask, jax.nn.softplus(z), 0.0)
        after = lax.cumsum(sp, axis=3, reverse=True) - sp
        a = jnp.where(mask, jnp.exp(jax.nn.log_sigmoid(z) - after), 0.0)
        return jnp.einsum('bhqk,bkhd->bqhd', a.astype(v.dtype), v)

    o = lax.map(block, (qblocks, pblocks))
    return jnp.moveaxis(o, 0, 1).reshape(Bsz, n * qb, H, D)[:, :Tq]


def ssd_chunked(x, dt, a, bm, cm, h0):
    Bsz, T, H, P = x.shape
    G, N = bm.shape[2], bm.shape[3]
    R = H // G
    L = SSD_CHUNK
    n = -(-T // L)
    pad = n * L - T
    x = _pad_time(x.astype(jnp.float32), pad).reshape(Bsz, n, L, G, R, P)
    dt = _pad_time(dt.astype(jnp.float32), pad).reshape(Bsz, n, L, G, R)
    bm = _pad_time(bm.astype(jnp.float32), pad).reshape(Bsz, n, L, G, N)
    cm = _pad_time(cm.astype(jnp.float32), pad).reshape(Bsz, n, L, G, N)
    acum = jnp.cumsum(dt * a.astype(jnp.float32).reshape(G, R), axis=2)
    causal = jnp.tril(jnp.ones((L, L), bool))[:, :, None, None]
    seg = acum[:, :, :, None] - acum[:, :, None, :]
    lmat = jnp.where(causal, jnp.exp(jnp.where(causal, seg, 0.0)), 0.0)
    xdt = x * dt[..., None]
    cb = jnp.einsum('bnigs,bnjgs->bnijg', cm, bm)
    y_diag = jnp.einsum('bnijg,bnijgr,bnjgrp->bnigrp', cb, lmat, xdt)
    decay_to_end = jnp.exp(acum[:, :, -1:] - acum)
    chunk_states = jnp.einsum('bnjgs,bnjgr,bnjgrp->bngrps', bm, decay_to_end, xdt)
    chunk_decay = jnp.exp(acum[:, :, -1])

    def step(h, xs):
        st, dec = xs
        return h * dec[..., None, None] + st, h

    h_final, h_in = lax.scan(step, h0.astype(jnp.float32).reshape(Bsz, G, R, P, N),
                             (jnp.moveaxis(chunk_states, 1, 0), jnp.moveaxis(chunk_decay, 1, 0)))
    h_in = jnp.moveaxis(h_in, 0, 1)
    y_off = jnp.einsum('bnigs,bngrps,bnigr->bnigrp', cm, h_in, jnp.exp(acum))
    y = (y_diag + y_off).reshape(Bsz, n * L, H, P)[:, :T]
    return y, h_final.reshape(Bsz, H, P, N)


def even_layer(x, past_k, past_v, conv_buf, s0, p):
    n_pre, n_post, w_in, conv_w, a_log, dt_bias, g_norm, sb_bias, w_out = p
    Bsz, T, _ = x.shape
    h = rmsnorm(x, n_pre)
    proj = jnp.einsum('btd,de->bte', h, w_in)
    qkv_a, a_a, b_a, z_a, q_b, k_b, v_b, z_b = _split(
        proj, [CONV_DIM_A, H_A, H_A, V_A, W_B, W_B, W_B, W_B])
    qkv_a, new_conv = causal_conv(qkv_a, conv_buf, conv_w)
    q_a, k_a, v_a = _split(qkv_a, [QK_A, QK_A, V_A])
    q_a = l2norm(q_a.reshape(Bsz, T, H_A, DK_A))
    k_a = l2norm(k_a.reshape(Bsz, T, H_A, DK_A))
    v_a = v_a.reshape(Bsz, T, H_A, DV_A)
    g = -jnp.exp(a_log.astype(jnp.float32)) * jax.nn.softplus(a_a.astype(jnp.float32) + dt_bias.astype(jnp.float32))
    beta = jax.nn.sigmoid(b_a.astype(jnp.float32))
    o_a, s_new = gated_delta_chunked(q_a, k_a, v_a, g, beta, s0)
    o_a = rmsnorm(o_a, g_norm).astype(x.dtype).reshape(Bsz, T, V_A) * jax.nn.silu(z_a)
    q_b = q_b.reshape(Bsz, T, H_B, D_B)
    k_b = k_b.reshape(Bsz, T, H_B, D_B)
    v_b = v_b.reshape(Bsz, T, H_B, D_B)
    if past_k is None:
        past = 0
        keys, vals = k_b, v_b
    else:
        past = past_k.shape[1]
        keys = jnp.concatenate([past_k.astype(k_b.dtype), k_b], axis=1)
        vals = jnp.concatenate([past_v.astype(v_b.dtype), v_b], axis=1)
    q_pos = past + jnp.arange(T, dtype=jnp.int32)
    k_pos = jnp.arange(past + T, dtype=jnp.int32)
    o_b = stick_breaking(q_b, keys, vals, sb_bias, q_pos, k_pos).reshape(Bsz, T, W_B) * jax.nn.silu(z_b)
    out = jnp.einsum('bte,ed->btd', jnp.concatenate([o_a, o_b], axis=-1), w_out)
    return x + rmsnorm(out, n_post), k_b, v_b, s_new, new_conv


def odd_layer(x, conv_buf, h0, p):
    n_pre, n_post, w_in, conv_w, conv_b, a_log, dt_bias, d_skip, g_norm, w_out = p
    Bsz, T, _ = x.shape
    h = rmsnorm(x, n_pre)
    proj = jnp.einsum('btd,de->bte', h, w_in)
    z, xbc, dt_raw = _split(proj, [D_INNER_C, CONV_DIM_C, H_C])
    xbc, new_conv = causal_conv(xbc, conv_buf, conv_w, conv_b)
    xs, bm, cm = _split(xbc, [D_INNER_C, N_GROUPS_C * D_STATE_C, N_GROUPS_C * D_STATE_C])
    xs = xs.reshape(Bsz, T, H_C, HEAD_DIM_C)
    bm = bm.reshape(Bsz, T, N_GROUPS_C, D_STATE_C)
    cm = cm.reshape(Bsz, T, N_GROUPS_C, D_STATE_C)
    dt = jax.nn.softplus(dt_raw.astype(jnp.float32) + dt_bias.astype(jnp.float32))
    a = -jnp.exp(a_log.astype(jnp.float32))
    y, h_new = ssd_chunked(xs, dt, a, bm, cm, h0)
    y = y + xs.astype(jnp.float32) * d_skip.astype(jnp.float32)[:, None]
    y = y.reshape(Bsz, T, D_INNER_C) * jax.nn.silu(z.astype(jnp.float32))
    y = rmsnorm(y.reshape(Bsz, T, N_GROUPS_C, D_INNER_C // N_GROUPS_C), g_norm.reshape(N_GROUPS_C, -1))
    y = y.reshape(Bsz, T, D_INNER_C).astype(x.dtype)
    out = jnp.einsum('bte,ed->btd', y, w_out)
    return x + rmsnorm(out, n_post), h_new, new_conv


def setup_inputs(seed: int = 0) -> dict:
    key = jax.random.key(seed)
    ks = iter(jax.random.split(key, 48))

    def nrm(shape, scale):
        return scale * jax.random.normal(next(ks), shape, jnp.float32)

    def gain(shape):
        return 1.0 + nrm(shape, 0.02)

    def a_log_init(shape):
        return jnp.log(jax.random.uniform(next(ks), shape, jnp.float32, 1.0, 16.0))

    def dt_bias_init(shape):
        dt = jnp.exp(jax.random.uniform(next(ks), shape, jnp.float32, np.log(1e-3), np.log(1e-1)))
        return dt + jnp.log(-jnp.expm1(-dt))

    n_pages = PAST_LEN // PAGE_SIZE
    n_used = DEC_BATCH * n_pages
    n_phys = n_used + max(1, n_used // 4)
    perm = jax.random.permutation(next(ks), n_phys)
    page_table = perm[:n_used].reshape(DEC_BATCH, n_pages).astype(jnp.int32)
    return {
        'x_prompt': nrm((BATCH, SEQ, D_MODEL), 1.0),
        'x_sample': nrm((DEC_BATCH, DEC_SEQ, D_MODEL), 1.0),
        'cache_sb_k': nrm((N_EVEN, n_phys, PAGE_SIZE, H_B, D_B), 1.0),
        'cache_sb_v': nrm((N_EVEN, n_phys, PAGE_SIZE, H_B, D_B), 1.0),
        'state_gdn': nrm((N_EVEN, DEC_BATCH, H_A, DK_A, DV_A), 0.3),
        'state_gdn_conv': nrm((N_EVEN, DEC_BATCH, CONV_W - 1, CONV_DIM_A), 1.0),
        'state_ssd': nrm((N_ODD, DEC_BATCH, H_C, HEAD_DIM_C, D_STATE_C), 0.3),
        'state_ssd_conv': nrm((N_ODD, DEC_BATCH, CONV_W - 1, CONV_DIM_C), 1.0),
        'page_table': page_table,
        'even_norm_pre': gain((N_EVEN, D_MODEL)),
        'even_norm_post': gain((N_EVEN, D_MODEL)),
        'even_w_in': nrm((N_EVEN, D_MODEL, IN_EVEN), D_MODEL ** -0.5),
        'gdn_conv_w': nrm((N_EVEN, CONV_W, CONV_DIM_A), 0.5),
        'gdn_a_log': a_log_init((N_EVEN, H_A)),
        'gdn_dt_bias': dt_bias_init((N_EVEN, H_A)),
        'gdn_norm_w': gain((N_EVEN, DV_A)),
        'sb_bias': SB_BIAS_INIT + nrm((N_EVEN, H_B), 0.1),
        'even_w_out': nrm((N_EVEN, MIX_EVEN, D_MODEL), MIX_EVEN ** -0.5),
        'odd_norm_pre': gain((N_ODD, D_MODEL)),
        'odd_norm_post': gain((N_ODD, D_MODEL)),
        'odd_w_in': nrm((N_ODD, D_MODEL, IN_ODD), D_MODEL ** -0.5),
        'ssd_conv_w': nrm((N_ODD, CONV_W, CONV_DIM_C), 0.5),
        'ssd_conv_b': nrm((N_ODD, CONV_DIM_C), 0.01),
        'ssd_a_log': a_log_init((N_ODD, H_C)),
        'ssd_dt_bias': dt_bias_init((N_ODD, H_C)),
        'ssd_d': 1.0 + nrm((N_ODD, H_C), 0.1),
        'ssd_norm_w': gain((N_ODD, D_INNER_C)),
        'odd_w_out': nrm((N_ODD, D_INNER_C, D_MODEL), D_INNER_C ** -0.5),
    }


def reference(x_prompt, x_sample, cache_sb_k, cache_sb_v, state_gdn, state_gdn_conv, state_ssd,
              state_ssd_conv, page_table, even_norm_pre, even_norm_post, even_w_in, gdn_conv_w, gdn_a_log,
              gdn_dt_bias, gdn_norm_w, sb_bias, even_w_out, odd_norm_pre, odd_norm_post, odd_w_in, ssd_conv_w,
              ssd_conv_b, ssd_a_log, ssd_dt_bias, ssd_d, ssd_norm_w, odd_w_out):
    bp = x_prompt.shape[0]
    bs = x_sample.shape[0]
    past_len = page_table.shape[1] * PAGE_SIZE
    yp, ys = x_prompt, x_sample
    kp, vp, kss, vss, gsp, gss, gcp, gcs, hsp, hss, hcp, hcs = ([] for _ in range(12))
    for layer in range(DEPTH):
        i = layer // 2
        if layer % 2 == 0:
            p = (even_norm_pre[i], even_norm_post[i], even_w_in[i], gdn_conv_w[i], gdn_a_log[i],
                 gdn_dt_bias[i], gdn_norm_w[i], sb_bias[i], even_w_out[i])
            yp, k_new, v_new, s_new, c_new = even_layer(
                yp, None, None, jnp.zeros((bp, CONV_W - 1, CONV_DIM_A), yp.dtype),
                jnp.zeros((bp, H_A, DK_A, DV_A), jnp.float32), p)
            kp.append(k_new); vp.append(v_new); gsp.append(s_new); gcp.append(c_new)
            past_k = cache_sb_k[i][page_table].reshape(bs, past_len, H_B, D_B)
            past_v = cache_sb_v[i][page_table].reshape(bs, past_len, H_B, D_B)
            ys, k_new, v_new, s_new, c_new = even_layer(
                ys, past_k, past_v, state_gdn_conv[i], state_gdn[i], p)
            kss.append(k_new); vss.append(v_new); gss.append(s_new); gcs.append(c_new)
        else:
            p = (odd_norm_pre[i], odd_norm_post[i], odd_w_in[i], ssd_conv_w[i], ssd_conv_b[i],
                 ssd_a_log[i], ssd_dt_bias[i], ssd_d[i], ssd_norm_w[i], odd_w_out[i])
            yp, h_new, c_new = odd_layer(
                yp, jnp.zeros((bp, CONV_W - 1, CONV_DIM_C), yp.dtype),
                jnp.zeros((bp, H_C, HEAD_DIM_C, D_STATE_C), jnp.float32), p)
            hsp.append(h_new); hcp.append(c_new)
            ys, h_new, c_new = odd_layer(ys, state_ssd_conv[i], state_ssd[i], p)
            hss.append(h_new); hcs.append(c_new)
    sb_k_prompt = jnp.stack(kp)
    sb_v_prompt = jnp.stack(vp)
    sb_k_sample = jnp.stack(kss)
    sb_v_sample = jnp.stack(vss)
    gdn_state_prompt = jnp.stack(gsp)
    gdn_state_sample = jnp.stack(gss)
    gdn_conv_prompt = jnp.stack(gcp)
    gdn_conv_sample = jnp.stack(gcs)
    ssd_state_prompt = jnp.stack(hsp)
    ssd_state_sample = jnp.stack(hss)
    ssd_conv_prompt = jnp.stack(hcp)
    ssd_conv_sample = jnp.stack(hcs)
    return (yp, ys, sb_k_prompt, sb_v_prompt, sb_k_sample, sb_v_sample, gdn_state_prompt, gdn_state_sample,
            gdn_conv_prompt, gdn_conv_sample, ssd_state_prompt, ssd_state_sample, ssd_conv_prompt, ssd_conv_sample)
```

```python
import functools

import jax
import jax.numpy as jnp
from jax import lax
from jax.experimental import pallas as pl
from jax.experimental.pallas import tpu as pltpu

F32 = jnp.float32
BF16 = jnp.bfloat16
HIGHEST = lax.Precision.HIGHEST

LANES = 128
SUBLANES = 8
VMEM_LIMIT_BYTES = 48 * 1024 * 1024

D_MODEL = 1024
H_A = 8
DK_A = 128
DV_A = 128
H_B = 8
D_B = 128
CONV_W = 4
GDN_CHUNK = 64
QK_A = H_A * DK_A
V_A = H_A * DV_A
W_B = H_B * D_B
CONV_DIM_A = 2 * QK_A + V_A
D_INNER_C = 2 * D_MODEL
HEAD_DIM_C = 64
H_C = D_INNER_C // HEAD_DIM_C
N_GROUPS_C = 4
D_STATE_C = 128
SSD_CHUNK = 64
CONV_DIM_C = D_INNER_C + 2 * N_GROUPS_C * D_STATE_C
PAGE_SIZE = 128
RMS_EPS = 1e-6
SEC = 1024


def _cparams(sem):
    return pltpu.CompilerParams(dimension_semantics=sem, vmem_limit_bytes=VMEM_LIMIT_BYTES)


def _dot(a, b):
    return jnp.dot(a, b, preferred_element_type=F32)


def _dot_nt(a, b):
    return lax.dot_general(a, b, (((1,), (1,)), ((), ())), preferred_element_type=F32)


def _dot_hi(a, b):
    return jnp.dot(a, b, preferred_element_type=F32, precision=HIGHEST)


def _sigmoid(x):
    return 1.0 / (1.0 + jnp.exp(-x))


def _silu(x):
    return x * _sigmoid(x)


def _softplus(x):
    return jnp.maximum(x, 0.0) + jnp.log1p(jnp.exp(-jnp.abs(x)))


def _rms(x, w):
    return x * lax.rsqrt(jnp.mean(x * x, axis=-1, keepdims=True) + RMS_EPS) * w


def _proj_body(x_ref, nw_ref, w_ref, ws_ref, o_ref, os_ref, h_ref):
    @pl.when(pl.program_id(1) == 0)
    def _():
        hb = _rms(x_ref[...], nw_ref[...]).astype(BF16)
        h_ref[...] = hb
        os_ref[...] = _dot(hb, ws_ref[...])

    o_ref[...] = _dot(h_ref[...], w_ref[...])


def _proj(x, nw, w_sec, w_small, tm):
    m, d = x.shape
    s = w_sec.shape[0]
    assert m % tm == 0
    return pl.pallas_call(
        _proj_body,
        grid=(m // tm, s),
        in_specs=[
            pl.BlockSpec((tm, d), lambda i, j: (i, 0)),
            pl.BlockSpec((1, d), lambda i, j: (0, 0)),
            pl.BlockSpec((None, d, SEC), lambda i, j: (j, 0, 0)),
            pl.BlockSpec((d, LANES), lambda i, j: (0, 0)),
        ],
        out_specs=[
            pl.BlockSpec((None, tm, SEC), lambda i, j: (j, i, 0)),
            pl.BlockSpec((tm, LANES), lambda i, j: (i, 0)),
        ],
        out_shape=[
            jax.ShapeDtypeStruct((s, m, SEC), F32),
            jax.ShapeDtypeStruct((m, LANES), F32),
        ],
        scratch_shapes=[pltpu.VMEM((tm, d), BF16)],
        compiler_params=_cparams(("parallel", "arbitrary")),
        name="proj",
    )(x, nw, w_sec, w_small)


def _outproj_body(ya_ref, yb_ref, wa_ref, wb_ref, nw_ref, x_ref, o_ref):
    acc = _dot(ya_ref[...].astype(BF16), wa_ref[...]) + _dot(yb_ref[...].astype(BF16), wb_ref[...])
    o_ref[...] = x_ref[...] + _rms(acc, nw_ref[...])


def _outproj(ya, yb, w, nw, x, tm, ya_col=0, yb_col=0):
    m, d = x.shape
    assert m % tm == 0
    return pl.pallas_call(
        _outproj_body,
        grid=(m // tm,),
        in_specs=[
            pl.BlockSpec((tm, SEC), lambda i: (i, ya_col)),
            pl.BlockSpec((tm, SEC), lambda i: (i, yb_col)),
            pl.BlockSpec((SEC, d), lambda i: (0, 0)),
            pl.BlockSpec((SEC, d), lambda i: (1, 0)),
            pl.BlockSpec((1, d), lambda i: (0, 0)),
            pl.BlockSpec((tm, d), lambda i: (i, 0)),
        ],
        out_specs=pl.BlockSpec((tm, d), lambda i: (i, 0)),
        out_shape=jax.ShapeDtypeStruct((m, d), F32),
        compiler_params=_cparams(("parallel",)),
        name="outproj",
    )(ya, yb, w, w, nw, x)


KEY_SEG = LANES // SUBLANES


def _sb_body(bias_ref, q_ref, zb_ref, kp_ref, vt_ref, o_ref, *, bq):
    h = pl.program_id(0)
    i = pl.program_id(1)
    nsub = bq // LANES
    qs = (q_ref[...] * (D_B ** -0.5)).astype(BF16)
    bias = bias_ref[h]
    seg = lax.broadcasted_iota(jnp.int32, (SUBLANES, bq), 0)
    lane = lax.broadcasted_iota(jnp.int32, (SUBLANES, bq), 1)

    def step(j, jj, carry):
        run_after, acc = carry
        kblk = kp_ref[pl.ds(pl.multiple_of(j * LANES, LANES), LANES), :]
        zt = _dot_nt(kblk, qs) + bias
        sp = _softplus(zt)
        zr = [zt[r * SUBLANES:(r + 1) * SUBLANES] for r in range(KEY_SEG)]
        spr = [sp[r * SUBLANES:(r + 1) * SUBLANES] for r in range(KEY_SEG)]
        masks = None
        if jj is not None:
            masks = [(jj * LANES + seg * KEY_SEG + r) < lane for r in range(KEY_SEG)]
            spr = [jnp.where(m, s_, 0.0) for m, s_ in zip(masks, spr)]
        cs = [None] * KEY_SEG
        run = spr[KEY_SEG - 1]
        cs[KEY_SEG - 1] = run
        for r in range(KEY_SEG - 2, -1, -1):
            run = run + spr[r]
            cs[r] = run
        inc = cs[0]
        for d in (1, 2, 4):
            sh = pltpu.roll(inc, SUBLANES - d, axis=0)
            inc = inc + jnp.where(seg + d < SUBLANES, sh, 0.0)
        off = jnp.where(seg < SUBLANES - 1, pltpu.roll(inc, SUBLANES - 1, axis=0), 0.0) + run_after
        a = [jnp.exp(zr[r] - (cs[r] + off)) for r in range(KEY_SEG)]
        if masks is not None:
            a = [jnp.where(m, a_, 0.0) for m, a_ in zip(masks, a)]
        a = jnp.concatenate(a, axis=0).astype(BF16)
        acc = acc + _dot(vt_ref[j], a)
        return run_after + inc[0:1], acc

    carry = (jnp.zeros((1, bq), F32), jnp.zeros((D_B, bq), F32))
    for jj in range(nsub - 1, -1, -1):
        carry = step(i * nsub + jj, jj, carry)

    def body(t, c):
        return step(i * nsub - 1 - t, None, c)

    _, acc = lax.fori_loop(0, i * nsub, body, carry)
    o_ref[...] = (acc.T * _silu(zb_ref[...])).astype(o_ref.dtype)


def _sb_prompt(bias, proj, q_sec, z_sec, kp, vt, bq):
    _, t, _ = proj.shape
    assert t % bq == 0 and bq % LANES == 0
    return pl.pallas_call(
        functools.partial(_sb_body, bq=bq),
        grid=(H_B, t // bq),
        in_specs=[
            pl.BlockSpec(memory_space=pltpu.SMEM),
            pl.BlockSpec((None, bq, D_B), lambda h, i: (q_sec, i, h)),
            pl.BlockSpec((None, bq, D_B), lambda h, i: (z_sec, i, h)),
            pl.BlockSpec((None, t, D_B), lambda h, i: (h, 0, 0)),
            pl.BlockSpec((None, t // LANES, D_B, LANES), lambda h, i: (h, 0, 0, 0)),
        ],
        out_specs=pl.BlockSpec((bq, D_B), lambda h, i: (i, h)),
        out_shape=jax.ShapeDtypeStruct((t, H_B * D_B), BF16),
        compiler_params=_cparams(("parallel", "arbitrary")),
        name="sb_prompt",
    )(bias, proj, proj, kp, vt)


def _sb_key_layouts(k, v):
    t = k.shape[0]
    nb = t // LANES

    def interleave(a):
        a = a.astype(BF16).reshape(nb, SUBLANES, KEY_SEG, H_B, D_B)
        return jnp.transpose(a, (3, 0, 2, 1, 4))

    kp = interleave(k).reshape(H_B, t, D_B)
    vt = jnp.swapaxes(interleave(v).reshape(H_B, nb, LANES, D_B), 2, 3)
    return kp, vt


def _conv_taps(xp_ref, w_ref, rows, lo, hi):
    acc = None
    for i in range(CONV_W):
        start = SUBLANES - (CONV_W - 1) + i
        term = xp_ref[start:start + rows, lo:hi] * w_ref[i:i + 1, lo:hi]
        acc = term if acc is None else acc + term
    return acc


def _tri_inverse_unit_lower(lmat, n):
    row = lax.broadcasted_iota(jnp.int32, (n, n), 0)
    col = lax.broadcasted_iota(jnp.int32, (n, n), 1)
    eye = jnp.where(row == col, 1.0, 0.0).astype(F32)
    m = -lmat
    inv = eye + m
    k = 2
    while k < n:
        m = _dot_hi(m, m)
        inv = _dot_hi(inv, eye + m)
        k *= 2
    return inv


def _gdn_body(q_ref, k_ref, v_ref, z_ref, ab_ref, cw_ref, alog_ref, dtb_ref, gn_ref,
              o_ref, s_out_ref, tail_ref, xp_ref, s_ref, *, cpb):
    c = GDN_CHUNK
    rows = cpb * c
    step = pl.program_id(0)

    @pl.when(step == 0)
    def _():
        xp_ref[0:SUBLANES, :] = jnp.zeros((SUBLANES, CONV_DIM_A), F32)
        s_ref[...] = jnp.zeros_like(s_ref)

    xp_ref[SUBLANES:, 0:QK_A] = q_ref[...]
    xp_ref[SUBLANES:, QK_A:2 * QK_A] = k_ref[...]
    xp_ref[SUBLANES:, 2 * QK_A:] = v_ref[...]

    ab = ab_ref[...]
    g_all = -jnp.exp(alog_ref[...]) * _softplus(ab + dtb_ref[...])
    beta_all = _sigmoid(ab)
    ri = lax.broadcasted_iota(jnp.int32, (c, c), 0)
    ci = lax.broadcasted_iota(jnp.int32, (c, c), 1)
    tril = ri >= ci
    strict = ri > ci
    tri_f = jnp.where(tril, 1.0, 0.0).astype(F32)

    for cc in range(cpb):
        r0 = cc * c
        gc = _dot_hi(tri_f, g_all[r0:r0 + c])
        gct = gc.T
        beta_c = beta_all[r0:r0 + c]
        for h in range(H_A):
            lo = h * DK_A
            sl = slice(SUBLANES + r0 - (CONV_W - 1), None)
            del sl
            qh = _silu(_conv_rows(xp_ref, cw_ref, r0, c, lo))
            kh = _silu(_conv_rows(xp_ref, cw_ref, r0, c, QK_A + lo))
            vh = _silu(_conv_rows(xp_ref, cw_ref, r0, c, 2 * QK_A + lo))
            qh = qh * lax.rsqrt(jnp.sum(qh * qh, axis=-1, keepdims=True) + 1e-6) * (DK_A ** -0.5)
            kh = kh * lax.rsqrt(jnp.sum(kh * kh, axis=-1, keepdims=True) + 1e-6)
            gcol = gc[:, h:h + 1]
            grow = gct[h:h + 1, :]
            beta = beta_c[:, H_A + h:H_A + h + 1]
            decay = jnp.where(tril, jnp.exp(jnp.where(tril, gcol - grow, 0.0)), 0.0)
            kb = kh.astype(BF16)
            kk = _dot_nt(kb, kb)
            lmat = jnp.where(strict, beta * kk * decay, 0.0)
            inv = _tri_inverse_unit_lower(lmat, c)
            egc = jnp.exp(gcol)
            rhs = jnp.concatenate([vh * beta, kh * (beta * egc)], axis=-1)
            sol = _dot_hi(inv, rhs)
            u = sol[:, :DV_A]
            w = sol[:, DV_A:]
            qk = jnp.where(tril, _dot_nt(qh.astype(BF16), kb) * decay, 0.0)
            glast = gcol[c - 1:c]
            kdec = kh * jnp.exp(glast - gcol)
            s_old = s_ref[h]
            sb = s_old.astype(BF16)
            v_new = u - _dot(w.astype(BF16), sb)
            vnb = v_new.astype(BF16)
            o = _dot((qh * egc).astype(BF16), sb) + _dot(qk.astype(BF16), vnb)
            s_ref[h] = s_old * jnp.exp(glast) + _dot(kdec.T.astype(BF16), vnb)
            o = _rms(o, gn_ref[...]) * _silu(z_ref[r0:r0 + c, lo:lo + DV_A])
            o_ref[r0:r0 + c, lo:lo + DV_A] = o.astype(o_ref.dtype)

    xp_ref[0:SUBLANES, :] = xp_ref[rows:rows + SUBLANES, :]

    @pl.when(step == pl.num_programs(0) - 1)
    def _():
        s_out_ref[...] = s_ref[...]
        tail_ref[...] = xp_ref[0:SUBLANES, :]


def _conv_rows(xp_ref, w_ref, r0, rows, lo):
    acc = None
    for i in range(CONV_W):
        start = SUBLANES - (CONV_W - 1) + i + r0
        term = xp_ref[start:start + rows, lo:lo + LANES] * w_ref[i:i + 1, lo:lo + LANES]
        acc = term if acc is None else acc + term
    return acc


def _gdn_prompt(proj, ab, conv_w, a_log_row, dt_bias_row, g_norm, cpb):
    _, t, _ = proj.shape
    rows = cpb * GDN_CHUNK
    assert t % rows == 0
    sec = lambda s: pl.BlockSpec((None, rows, SEC), lambda i: (s, i, 0))
    full = lambda shape: pl.BlockSpec(shape, lambda i: (0,) * len(shape))
    return pl.pallas_call(
        functools.partial(_gdn_body, cpb=cpb),
        grid=(t // rows,),
        in_specs=[sec(0), sec(1), sec(2), sec(3),
                  pl.BlockSpec((rows, LANES), lambda i: (i, 0)),
                  full((CONV_W, CONV_DIM_A)), full((1, LANES)), full((1, LANES)), full((1, DV_A))],
        out_specs=[pl.BlockSpec((rows, V_A), lambda i: (i, 0)),
                   full((H_A, DK_A, DV_A)), full((SUBLANES, CONV_DIM_A))],
        out_shape=[jax.ShapeDtypeStruct((t, V_A), BF16),
                   jax.ShapeDtypeStruct((H_A, DK_A, DV_A), F32),
                   jax.ShapeDtypeStruct((SUBLANES, CONV_DIM_A), F32)],
        scratch_shapes=[pltpu.VMEM((rows + SUBLANES, CONV_DIM_A), F32),
                        pltpu.VMEM((H_A, DK_A, DV_A), F32)],
        compiler_params=_cparams(("arbitrary",)),
        name="gdn_prompt",
    )(proj, proj, proj, proj, ab, conv_w, a_log_row, dt_bias_row, g_norm)


def _ssd_body(z0_ref, z1_ref, x0_ref, x1_ref, bc_ref, dt_ref, cw_ref, cb_ref, alog_ref, dtb_ref,
              dskip_ref, gn_ref, exp_ref, o_ref, s_out_ref, tail_ref, xp_ref, s_ref, *, cpb):
    c = SSD_CHUNK
    rows = cpb * c
    step = pl.program_id(0)
    gw = D_INNER_C // N_GROUPS_C
    n_bc = N_GROUPS_C * D_STATE_C

    @pl.when(step == 0)
    def _():
        xp_ref[0:SUBLANES, :] = jnp.zeros((SUBLANES, CONV_DIM_C), F32)
        s_ref[...] = jnp.zeros_like(s_ref)

    xp_ref[SUBLANES:, 0:SEC] = x0_ref[...]
    xp_ref[SUBLANES:, SEC:2 * SEC] = x1_ref[...]
    xp_ref[SUBLANES:, 2 * SEC:] = bc_ref[...]

    dt_all = _softplus(dt_ref[...] + dtb_ref[...])
    da_all = dt_all * (-jnp.exp(alog_ref[...]))
    ri = lax.broadcasted_iota(jnp.int32, (c, c), 0)
    ci = lax.broadcasted_iota(jnp.int32, (c, c), 1)
    tril = ri >= ci
    tri_f = jnp.where(tril, 1.0, 0.0).astype(F32)
    lane = lax.broadcasted_iota(jnp.int32, (1, LANES), 1)
    left = lane < HEAD_DIM_C
    rowi = lax.broadcasted_iota(jnp.int32, (LANES, 1), 0)
    top = rowi < HEAD_DIM_C
    expand = exp_ref[...]

    for cc in range(cpb):
        r0 = cc * c
        dt = dt_all[r0:r0 + c]
        acum = _dot_hi(tri_f, da_all[r0:r0 + c])
        acum_t = acum.T
        dt_x = _dot_hi(dt, expand)
        ac_x = _dot_hi(acum, expand)
        for g in range(N_GROUPS_C):
            bm = _silu(_conv_rows(xp_ref, cw_ref, r0, c, D_INNER_C + g * D_STATE_C)
                       + cb_ref[:, D_INNER_C + g * D_STATE_C:D_INNER_C + (g + 1) * D_STATE_C])
            cm = _silu(_conv_rows(xp_ref, cw_ref, r0, c, D_INNER_C + n_bc + g * D_STATE_C)
                       + cb_ref[:, D_INNER_C + n_bc + g * D_STATE_C:D_INNER_C + n_bc + (g + 1) * D_STATE_C])
            bmb = bm.astype(BF16)
            cmb = cm.astype(BF16)
            cb = _dot_nt(cmb, bmb)
            ys = []
            for pr in range(gw // LANES):
                lo = g * gw + pr * LANES
                h0 = lo // HEAD_DIM_C
                xs = _silu(_conv_rows(xp_ref, cw_ref, r0, c, lo) + cb_ref[:, lo:lo + LANES])
                xdt = xs * dt_x[:, lo:lo + LANES]
                acx = ac_x[:, lo:lo + LANES]
                y = None
                for hh in range(2):
                    hd = h0 + hh
                    seg = acum[:, hd:hd + 1] - acum_t[hd:hd + 1, :]
                    lm = jnp.where(tril, jnp.exp(jnp.where(tril, seg, 0.0)), 0.0)
                    xm = jnp.where(left if hh == 0 else jnp.logical_not(left), xdt, 0.0)
                    term = _dot((cb * lm).astype(BF16), xm.astype(BF16))
                    y = term if y is None else y + term
                s_old = s_ref[lo:lo + LANES, :]
                y = y + _dot_nt(cmb, s_old.astype(BF16)) * jnp.exp(acx)
                dte = jnp.exp(acx[c - 1:c] - acx)
                st = _dot((xdt * dte).T.astype(BF16), bmb)
                al0 = acum[c - 1:c, h0:h0 + 1]
                al1 = acum[c - 1:c, h0 + 1:h0 + 2]
                s_ref[lo:lo + LANES, :] = s_old * jnp.exp(jnp.where(top, al0, al1)) + st
                y = y + xs * dskip_ref[:, lo:lo + LANES]
                zz = (z0_ref if lo < SEC else z1_ref)[r0:r0 + c, (lo % SEC):(lo % SEC) + LANES]
                ys.append(y * _silu(zz))
            yg = jnp.concatenate(ys, axis=-1)
            yg = _rms(yg, gn_ref[:, g * gw:(g + 1) * gw])
            o_ref[r0:r0 + c, g * gw:(g + 1) * gw] = yg.astype(o_ref.dtype)

    xp_ref[0:SUBLANES, :] = xp_ref[rows:rows + SUBLANES, :]

    @pl.when(step == pl.num_programs(0) - 1)
    def _():
        s_out_ref[...] = s_ref[...]
        tail_ref[...] = xp_ref[0:SUBLANES, :]


def _ssd_prompt(proj, dt_raw, conv_w, conv_b, a_log_row, dt_bias_row, dskip_x, g_norm, expand, cpb):
    _, t, _ = proj.shape
    rows = cpb * SSD_CHUNK
    assert t % rows == 0
    sec = lambda s: pl.BlockSpec((None, rows, SEC), lambda i: (s, i, 0))
    full = lambda shape: pl.BlockSpec(shape, lambda i: (0,) * len(shape))
    return pl.pallas_call(
        functools.partial(_ssd_body, cpb=cpb),
        grid=(t // rows,),
        in_specs=[sec(0), sec(1), sec(2), sec(3), sec(4),
                  pl.BlockSpec((rows, LANES), lambda i: (i, 0)),
                  full((CONV_W, CONV_DIM_C)), full((1, CONV_DIM_C)), full((1, LANES)), full((1, LANES)),
                  full((1, D_INNER_C)), full((1, D_INNER_C)), full((LANES, D_INNER_C))],
        out_specs=[pl.BlockSpec((rows, D_INNER_C), lambda i: (i, 0)),
                   full((D_INNER_C, D_STATE_C)), full((SUBLANES, CONV_DIM_C))],
        out_shape=[jax.ShapeDtypeStruct((t, D_INNER_C), BF16),
                   jax.ShapeDtypeStruct((D_INNER_C, D_STATE_C), F32),
                   jax.ShapeDtypeStruct((SUBLANES, CONV_DIM_C), F32)],
        scratch_shapes=[pltpu.VMEM((rows + SUBLANES, CONV_DIM_C), F32),
                        pltpu.VMEM((D_INNER_C, D_STATE_C), F32)],
        compiler_params=_cparams(("arbitrary",)),
        name="ssd_prompt",
    )(proj, proj, proj, proj, proj, dt_raw, conv_w, conv_b, a_log_row, dt_bias_row, dskip_x, g_norm, expand)


def _head_expand():
    r = jnp.arange(LANES)[:, None]
    cidx = jnp.arange(D_INNER_C)[None, :]
    return (cidx // HEAD_DIM_C == r).astype(F32)


def _row_to_col(row, eye):
    return jnp.sum(jnp.where(eye, row, 0.0), axis=1, keepdims=True)


def _col_to_row(col, eye):
    return jnp.sum(jnp.where(eye, col, 0.0), axis=0, keepdims=True)


def _eye(n):
    return lax.broadcasted_iota(jnp.int32, (n, n), 0) == lax.broadcasted_iota(jnp.int32, (n, n), 1)


def _conv_step(x_row, buf_ref, w_ref, lo, hi):
    acc = x_row * w_ref[CONV_W - 1:CONV_W, lo:hi]
    for i in range(CONV_W - 1):
        acc = acc + buf_ref[i:i + 1, lo:hi] * w_ref[i:i + 1, lo:hi]
    return acc


def _gdn_step_body(q_ref, k_ref, v_ref, z_ref, ab_ref, buf_ref, s_ref, cw_ref, alog_ref, dtb_ref, gn_ref,
                   o_ref, s_out_ref, buf_out_ref):
    eye = _eye(LANES)
    ab = ab_ref[...]
    eg_all = jnp.exp(-jnp.exp(alog_ref[...]) * _softplus(ab + dtb_ref[...]))
    beta_all = _sigmoid(ab)
    buf_out_ref[0:CONV_W - 2, :] = buf_ref[1:CONV_W - 1, :]
    pre = (q_ref, k_ref, v_ref)
    for n in range(3):
        buf_out_ref[CONV_W - 2:CONV_W - 1, n * SEC:(n + 1) * SEC] = pre[n][...]
    for h in range(H_A):
        lo = h * DK_A
        q = _silu(_conv_step(q_ref[:, lo:lo + DK_A], buf_ref, cw_ref, lo, lo + DK_A))
        k = _silu(_conv_step(k_ref[:, lo:lo + DK_A], buf_ref, cw_ref, QK_A + lo, QK_A + lo + DK_A))
        v = _silu(_conv_step(v_ref[:, lo:lo + DV_A], buf_ref, cw_ref, 2 * QK_A + lo, 2 * QK_A + lo + DV_A))
        q = q * lax.rsqrt(jnp.sum(q * q, axis=-1, keepdims=True) + 1e-6) * (DK_A ** -0.5)
        k = k * lax.rsqrt(jnp.sum(k * k, axis=-1, keepdims=True) + 1e-6)
        eg = eg_all[:, h:h + 1]
        beta = beta_all[:, H_A + h:H_A + h + 1]
        s0 = s_ref[h]
        kc = _row_to_col(k, eye)
        qc = _row_to_col(q, eye)
        sk = jnp.sum(s0 * kc, axis=0, keepdims=True)
        sq = jnp.sum(s0 * qc, axis=0, keepdims=True)
        v_new = beta * (v - eg * sk)
        o = eg * sq + jnp.sum(q * k, axis=-1, keepdims=True) * v_new
        s_out_ref[h] = s0 * eg + kc * v_new
        o_ref[:, lo:lo + DV_A] = _rms(o, gn_ref[...]) * _silu(z_ref[:, lo:lo + DV_A])


def _gdn_step(proj, ab, conv_buf, state, conv_w, a_log_row, dt_bias_row, g_norm):
    b = ab.shape[0]
    sec = lambda s: pl.BlockSpec((None, None, 1, SEC), lambda i: (s, i, 0, 0))
    full = lambda shape: pl.BlockSpec(shape, lambda i: (0,) * len(shape))
    return pl.pallas_call(
        _gdn_step_body,
        grid=(b,),
        in_specs=[sec(0), sec(1), sec(2), sec(3),
                  pl.BlockSpec((None, 1, LANES), lambda i: (i, 0, 0)),
                  pl.BlockSpec((None, CONV_W - 1, CONV_DIM_A), lambda i: (i, 0, 0)),
                  pl.BlockSpec((None, H_A, DK_A, DV_A), lambda i: (i, 0, 0, 0)),
                  full((CONV_W, CONV_DIM_A)), full((1, LANES)), full((1, LANES)), full((1, DV_A))],
        out_specs=[pl.BlockSpec((None, 1, V_A), lambda i: (i, 0, 0)),
                   pl.BlockSpec((None, H_A, DK_A, DV_A), lambda i: (i, 0, 0, 0)),
                   pl.BlockSpec((None, CONV_W - 1, CONV_DIM_A), lambda i: (i, 0, 0))],
        out_shape=[jax.ShapeDtypeStruct((b, 1, V_A), F32),
                   jax.ShapeDtypeStruct((b, H_A, DK_A, DV_A), F32),
                   jax.ShapeDtypeStruct((b, CONV_W - 1, CONV_DIM_A), F32)],
        compiler_params=_cparams(("parallel",)),
        name="gdn_step",
    )(proj, proj, proj, proj, ab, conv_buf, state, conv_w, a_log_row, dt_bias_row, g_norm)


def _ssd_step_body(z0_ref, z1_ref, x0_ref, x1_ref, bc_ref, dt_ref, buf_ref, s_ref, cw_ref, cb_ref, alog_ref,
                   dtb_ref, dskip_ref, gn_ref, o_ref, s_out_ref, buf_out_ref):
    eye = _eye(LANES)
    gw = D_INNER_C // N_GROUPS_C
    n_bc = N_GROUPS_C * D_STATE_C
    top = lax.broadcasted_iota(jnp.int32, (LANES, 1), 0) < HEAD_DIM_C
    dt_all = _softplus(dt_ref[...] + dtb_ref[...])
    dec_all = jnp.exp(dt_all * (-jnp.exp(alog_ref[...])))
    buf_out_ref[0:CONV_W - 2, :] = buf_ref[1:CONV_W - 1, :]
    pre = (x0_ref, x1_ref, bc_ref)
    for n in range(3):
        buf_out_ref[CONV_W - 2:CONV_W - 1, n * SEC:(n + 1) * SEC] = pre[n][...]

    def conv(ref, lo_in, lo):
        return _silu(_conv_step(ref[:, lo_in:lo_in + LANES], buf_ref, cw_ref, lo, lo + LANES)
                     + cb_ref[:, lo:lo + LANES])

    for g in range(N_GROUPS_C):
        bm = conv(bc_ref, g * D_STATE_C, D_INNER_C + g * D_STATE_C)
        cm = conv(bc_ref, n_bc + g * D_STATE_C, D_INNER_C + n_bc + g * D_STATE_C)
        ys = []
        for pr in range(gw // LANES):
            lo = g * gw + pr * LANES
            h0 = lo // HEAD_DIM_C
            xs = conv(x0_ref if lo < SEC else x1_ref, lo % SEC, lo)
            xc = _row_to_col(xs, eye)
            dt_c = jnp.where(top, dt_all[:, h0:h0 + 1], dt_all[:, h0 + 1:h0 + 2])
            dec_c = jnp.where(top, dec_all[:, h0:h0 + 1], dec_all[:, h0 + 1:h0 + 2])
            h_new = s_ref[lo:lo + LANES, :] * dec_c + (xc * dt_c) * bm
            s_out_ref[lo:lo + LANES, :] = h_new
            y = _col_to_row(jnp.sum(h_new * cm, axis=1, keepdims=True), eye)
            y = y + xs * dskip_ref[:, lo:lo + LANES]
            zz = (z0_ref if lo < SEC else z1_ref)[:, (lo % SEC):(lo % SEC) + LANES]
            ys.append(y * _silu(zz))
        yg = jnp.concatenate(ys, axis=-1)
        o_ref[:, g * gw:(g + 1) * gw] = _rms(yg, gn_ref[:, g * gw:(g + 1) * gw])


def _ssd_step(proj, dt_raw, conv_buf, state, conv_w, conv_b, a_log_row, dt_bias_row, dskip_x, g_norm):
    b = dt_raw.shape[0]
    sec = lambda s: pl.BlockSpec((None, None, 1, SEC), lambda i: (s, i, 0, 0))
    full = lambda shape: pl.BlockSpec(shape, lambda i: (0,) * len(shape))
    return pl.pallas_call(
        _ssd_step_body,
        grid=(b,),
        in_specs=[sec(0), sec(1), sec(2), sec(3), sec(4),
                  pl.BlockSpec((None, 1, LANES), lambda i: (i, 0, 0)),
                  pl.BlockSpec((None, CONV_W - 1, CONV_DIM_C), lambda i: (i, 0, 0)),
                  pl.BlockSpec((None, D_INNER_C, D_STATE_C), lambda i: (i, 0, 0)),
                  full((CONV_W, CONV_DIM_C)), full((1, CONV_DIM_C)), full((1, LANES)), full((1, LANES)),
                  full((1, D_INNER_C)), full((1, D_INNER_C))],
        out_specs=[pl.BlockSpec((None, 1, D_INNER_C), lambda i: (i, 0, 0)),
                   pl.BlockSpec((None, D_INNER_C, D_STATE_C), lambda i: (i, 0, 0)),
                   pl.BlockSpec((None, CONV_W - 1, CONV_DIM_C), lambda i: (i, 0, 0))],
        out_shape=[jax.ShapeDtypeStruct((b, 1, D_INNER_C), F32),
                   jax.ShapeDtypeStruct((b, D_INNER_C, D_STATE_C), F32),
                   jax.ShapeDtypeStruct((b, CONV_W - 1, CONV_DIM_C), F32)],
        compiler_params=_cparams(("parallel",)),
        name="ssd_step",
    )(proj, proj, proj, proj, proj, dt_raw, conv_buf, state, conv_w, conv_b, a_log_row, dt_bias_row,
      dskip_x, g_norm)


SB_PAGES_PER_STEP = 4


def _sb_paged_body(pt_ref, q_ref, z_ref, bias_ref, *refs, ppb):
    del pt_ref
    k_refs = refs[:ppb]
    v_refs = refs[ppb:2 * ppb]
    o_ref, acc_ref, run_ref = refs[2 * ppb:]
    p = pl.program_id(1)
    grp = PAGE_SIZE // KEY_SEG
    rows = KEY_SEG * H_B

    @pl.when(p == 0)
    def _():
        acc_ref[...] = jnp.zeros_like(acc_ref)
        run_ref[...] = jnp.zeros_like(run_ref)

    sub = lax.broadcasted_iota(jnp.int32, (SUBLANES, LANES), 0)
    lane = lax.broadcasted_iota(jnp.int32, (SUBLANES, LANES), 1)
    own = (lane % H_B) == sub
    ri = lax.broadcasted_iota(jnp.int32, (rows, 2 * rows), 0)
    ci = lax.broadcasted_iota(jnp.int32, (rows, 2 * rows), 1)
    same_head = (ri % H_B) == (ci % H_B)
    tmat = jnp.where(same_head & ((ci >= rows) | (ri // H_B >= ci // H_B)), 1.0, 0.0).astype(BF16)
    qb = q_ref[...].astype(BF16)
    bias = bias_ref[...]
    acc = acc_ref[...]
    run_after = run_ref[...]
    for r in range(ppb):
        k2 = k_refs[r][...].reshape(PAGE_SIZE * H_B, D_B).astype(BF16)
        v2 = v_refs[r][...].reshape(PAGE_SIZE * H_B, D_B).astype(BF16)
        zrows = []
        for g in range(grp):
            zt = _dot_nt(qb, k2[g * rows:(g + 1) * rows])
            zrows.append(jnp.sum(jnp.where(own, zt, 0.0), axis=0, keepdims=True))
        z = jnp.concatenate(zrows, axis=0) * (D_B ** -0.5) + bias
        sp = _softplus(z)
        hi = sp.astype(BF16)
        lo = (sp - hi.astype(F32)).astype(BF16)
        cs = _dot(hi, tmat) + _dot(lo, tmat)
        inc = cs[:, rows:]
        for d in (1, 2, 4):
            sh = pltpu.roll(inc, SUBLANES - d, axis=0)
            inc = inc + jnp.where(sub + d < SUBLANES, sh, 0.0)
        off = jnp.where(sub < SUBLANES - 1, pltpu.roll(inc, SUBLANES - 1, axis=0), 0.0) + run_after
        a = jnp.exp(z - (cs[:, :rows] + off))
        lhs = jnp.concatenate([jnp.where(own, a[g:g + 1], 0.0) for g in range(grp)], axis=1)
        acc = acc + _dot(lhs.astype(BF16), v2)
        run_after = run_after + inc[0:1]
    acc_ref[...] = acc
    run_ref[...] = run_after

    @pl.when(p == pl.num_programs(1) - 1)
    def _():
        o_ref[...] = acc * _silu(z_ref[...])


def _sb_paged(page_table, q, zgate, bias_row, cache_k, cache_v, layer):
    b, npg = page_table.shape
    ppb = SB_PAGES_PER_STEP
    assert npg % ppb == 0

    def page_spec(r):
        return pl.BlockSpec((None, None, PAGE_SIZE, H_B, D_B),
                            lambda i, p, pt: (layer, pt[i, npg - 1 - (p * ppb + r)], 0, 0, 0))

    row = pl.BlockSpec((None, H_B, D_B), lambda i, p, pt: (i, 0, 0))
    grid_spec = pltpu.PrefetchScalarGridSpec(
        num_scalar_prefetch=1,
        grid=(b, npg // ppb),
        in_specs=[row, row, pl.BlockSpec((1, LANES), lambda i, p, pt: (0, 0))]
        + [page_spec(r) for r in range(ppb)] * 2,
        out_specs=row,
        scratch_shapes=[pltpu.VMEM((H_B, D_B), F32), pltpu.VMEM((1, LANES), F32)],
    )
    return pl.pallas_call(
        functools.partial(_sb_paged_body, ppb=ppb),
        grid_spec=grid_spec,
        out_shape=jax.ShapeDtypeStruct((b, H_B, D_B), F32),
        compiler_params=_cparams(("parallel", "arbitrary")),
        name="sb_paged",
    )(page_table, q, zgate, bias_row, *([cache_k] * ppb), *([cache_v] * ppb))


def _pad_lanes(v):
    return jnp.pad(v.astype(F32), (0, LANES - v.shape[0])).reshape(1, LANES)


def _even_weights(w_in):
    o_ab = CONV_DIM_A
    o_z = o_ab + 2 * H_A
    w_main = jnp.concatenate([w_in[:, :o_ab], w_in[:, o_z:]], axis=1).astype(BF16)
    w_sec = jnp.transpose(w_main.reshape(D_MODEL, 8, SEC), (1, 0, 2))
    w_small = jnp.pad(w_in[:, o_ab:o_z], ((0, 0), (0, LANES - 2 * H_A))).astype(BF16)
    return w_sec, w_small


def _odd_weights(w_in):
    n_main = D_INNER_C + CONV_DIM_C
    w_sec = jnp.transpose(w_in[:, :n_main].astype(BF16).reshape(D_MODEL, 5, SEC), (1, 0, 2))
    w_small = jnp.pad(w_in[:, n_main:], ((0, 0), (0, LANES - H_C))).astype(BF16)
    return w_sec, w_small


def kernel(x_prompt, x_sample, cache_sb_k, cache_sb_v, state_gdn, state_gdn_conv, state_ssd, state_ssd_conv, page_table, even_norm_pre, even_norm_post, even_w_in, gdn_conv_w, gdn_a_log, gdn_dt_bias, gdn_norm_w, sb_bias, even_w_out, odd_norm_pre, odd_norm_post, odd_w_in, ssd_conv_w, ssd_conv_b, ssd_a_log, ssd_dt_bias, ssd_d, ssd_norm_w, odd_w_out):
    t = x_prompt.shape[1]
    bs = x_sample.shape[0]
    xp = x_prompt[0]
    xs = x_sample[:, 0]
    tail = SUBLANES - (CONV_W - 1)
    tm_p = 512
    tm_s = bs

    w_sec, w_small = _even_weights(even_w_in[0])
    w_out_e = even_w_out[0].astype(BF16)
    a_log_e = _pad_lanes(gdn_a_log[0])
    dt_bias_e = _pad_lanes(gdn_dt_bias[0])
    proj, ab = _proj(xp, even_norm_pre, w_sec, w_small, tm_p)
    sb_k_prompt = proj[5].reshape(1, 1, t, H_B, D_B)
    sb_v_prompt = proj[6].reshape(1, 1, t, H_B, D_B)
    kp, vt = _sb_key_layouts(proj[5], proj[6])
    o_b = _sb_prompt(sb_bias[0], proj, 4, 7, kp, vt, 256)
    o_a, gdn_s, gdn_tail = _gdn_prompt(proj, ab, gdn_conv_w[0], a_log_e, dt_bias_e, gdn_norm_w, 2)
    y1 = _outproj(o_a, o_b, w_out_e, even_norm_post, xp, tm_p)
    proj_s, ab_s = _proj(xs, even_norm_pre, w_sec, w_small, tm_s)
    sb_k_sample = proj_s[5].reshape(1, bs, 1, H_B, D_B)
    sb_v_sample = proj_s[6].reshape(1, bs, 1, H_B, D_B)
    o_b_s = _sb_paged(page_table, proj_s[4].reshape(bs, H_B, D_B), proj_s[7].reshape(bs, H_B, D_B),
                      jnp.tile(sb_bias[0].astype(F32), LANES // H_B).reshape(1, LANES),
                      cache_sb_k, cache_sb_v, 0)
    o_a_s, gdn_s_s, gdn_conv_s = _gdn_step(proj_s.reshape(8, bs, 1, SEC), ab_s.reshape(bs, 1, LANES),
                                           state_gdn_conv[0], state_gdn[0], gdn_conv_w[0], a_log_e,
                                           dt_bias_e, gdn_norm_w)
    y1_s = _outproj(o_a_s.reshape(bs, V_A), o_b_s.reshape(bs, W_B), w_out_e, even_norm_post, xs, tm_s)

    w_sec_o, w_small_o = _odd_weights(odd_w_in[0])
    w_out_o = odd_w_out[0].astype(BF16)
    a_log_o = _pad_lanes(ssd_a_log[0])
    dt_bias_o = _pad_lanes(ssd_dt_bias[0])
    dskip_x = jnp.repeat(ssd_d[0].astype(F32), HEAD_DIM_C).reshape(1, D_INNER_C)
    proj_o, dt_raw = _proj(y1, odd_norm_pre, w_sec_o, w_small_o, tm_p)
    y_ssd, ssd_s, ssd_tail = _ssd_prompt(proj_o, dt_raw, ssd_conv_w[0], ssd_conv_b, a_log_o, dt_bias_o,
                                         dskip_x, ssd_norm_w, _head_expand(), 2)
    y2 = _outproj(y_ssd, y_ssd, w_out_o, odd_norm_post, y1, tm_p, 0, 1)
    proj_os, dt_raw_s = _proj(y1_s, odd_norm_pre, w_sec_o, w_small_o, tm_s)
    y_ssd_s, ssd_s_s, ssd_conv_s = _ssd_step(proj_os.reshape(5, bs, 1, SEC), dt_raw_s.reshape(bs, 1, LANES),
                                             state_ssd_conv[0], state_ssd[0].reshape(bs, D_INNER_C, D_STATE_C),
                                             ssd_conv_w[0], ssd_conv_b, a_log_o, dt_bias_o, dskip_x, ssd_norm_w)
    y_ssd_s = y_ssd_s.reshape(bs, D_INNER_C)
    y2_s = _outproj(y_ssd_s, y_ssd_s, w_out_o, odd_norm_post, y1_s, tm_s, 0, 1)

    return (y2[None], y2_s[:, None], sb_k_prompt, sb_v_prompt, sb_k_sample, sb_v_sample,
            gdn_s[None, None], gdn_s_s[None], gdn_tail[tail:][None, None], gdn_conv_s[None],
            ssd_s.reshape(1, 1, H_C, HEAD_DIM_C, D_STATE_C),
            ssd_s_s.reshape(1, bs, H_C, HEAD_DIM_C, D_STATE_C),
            ssd_tail[tail:][None, None], ssd_conv_s[None])
```

```python
import functools

import jax
import jax.numpy as jnp
from jax import lax
from jax.experimental import pallas as pl
from jax.experimental.pallas import tpu as pltpu

F32 = jnp.float32
BF16 = jnp.bfloat16
HIGHEST = lax.Precision.HIGHEST

LANES = 128
SUBLANES = 8
VMEM_LIMIT_BYTES = 48 * 1024 * 1024

D_MODEL = 1024
H_A = 8
DK_A = 128
DV_A = 128
H_B = 8
D_B = 128
CONV_W = 4
GDN_CHUNK = 64
QK_A = H_A * DK_A
V_A = H_A * DV_A
W_B = H_B * D_B
CONV_DIM_A = 2 * QK_A + V_A
D_INNER_C = 2 * D_MODEL
HEAD_DIM_C = 64
H_C = D_INNER_C // HEAD_DIM_C
N_GROUPS_C = 4
D_STATE_C = 128
SSD_CHUNK = 64
CONV_DIM_C = D_INNER_C + 2 * N_GROUPS_C * D_STATE_C
PAGE_SIZE = 128
RMS_EPS = 1e-6
SEC = 1024


def _cparams(sem):
    return pltpu.CompilerParams(dimension_semantics=sem, vmem_limit_bytes=VMEM_LIMIT_BYTES)


def _dot(a, b):
    return jnp.dot(a, b, preferred_element_type=F32)


def _dot_nt(a, b):
    return lax.dot_general(a, b, (((1,), (1,)), ((), ())), preferred_element_type=F32)


def _dot_hi(a, b):
    return jnp.dot(a, b, preferred_element_type=F32, precision=HIGHEST)


def _split3_bf16(x):
    hi = x.astype(BF16)
    r1 = x - hi.astype(F32)
    mid = r1.astype(BF16)
    return hi, mid, (r1 - mid.astype(F32)).astype(BF16)


def _select_dot(x, sel):
    hi, mid, lo = _split3_bf16(x)
    return _dot(hi, sel) + (_dot(mid, sel) + _dot(lo, sel))


def _sigmoid(x):
    return 1.0 / (1.0 + jnp.exp(-x))


def _silu(x):
    return x * _sigmoid(x)


def _softplus(x):
    return jnp.maximum(x, 0.0) + jnp.log1p(jnp.exp(-jnp.abs(x)))


def _rms(x, w):
    return x * lax.rsqrt(jnp.mean(x * x, axis=-1, keepdims=True) + RMS_EPS) * w


def _proj_body(x_ref, nw_ref, w_ref, ws_ref, o_ref, os_ref, *rest, own):
    own_refs, h_ref = rest[:-1], rest[-1]
    j = pl.program_id(1)

    @pl.when(j == 0)
    def _():
        hb = _rms(x_ref[...], nw_ref[...]).astype(BF16)
        h_ref[...] = hb
        os_ref[...] = _dot(hb, ws_ref[...])

    res = _dot(h_ref[...], w_ref[...])
    stacked = None
    for sec, ref in zip(own, own_refs):
        @pl.when(j == sec)
        def _(ref=ref):
            ref[...] = res

        stacked = (j != sec) if stacked is None else stacked & (j != sec)
    if stacked is None:
        o_ref[...] = res
    else:
        @pl.when(stacked)
        def _():
            o_ref[...] = res


def _proj(x, nw, w_sec, w_small, tm, own=()):
    m, d = x.shape
    s = w_sec.shape[0]
    assert m % tm == 0

    def stacked_index(j):
        return j - sum((j >= e).astype(jnp.int32) for e in own) if own else j

    own_spec = pl.BlockSpec((tm, SEC), lambda i, j: (i, 0))
    return pl.pallas_call(
        functools.partial(_proj_body, own=own),
        grid=(m // tm, s),
        in_specs=[
            pl.BlockSpec((tm, d), lambda i, j: (i, 0)),
            pl.BlockSpec((1, d), lambda i, j: (0, 0)),
            pl.BlockSpec((None, d, SEC), lambda i, j: (j, 0, 0)),
            pl.BlockSpec((d, LANES), lambda i, j: (0, 0)),
        ],
        out_specs=[
            pl.BlockSpec((None, tm, SEC), lambda i, j: (stacked_index(j), i, 0)),
            pl.BlockSpec((tm, LANES), lambda i, j: (i, 0)),
        ] + [own_spec] * len(own),
        out_shape=[
            jax.ShapeDtypeStruct((s - len(own), m, SEC), F32),
            jax.ShapeDtypeStruct((m, LANES), F32),
        ] + [jax.ShapeDtypeStruct((m, SEC), F32)] * len(own),
        scratch_shapes=[pltpu.VMEM((tm, d), BF16)],
        compiler_params=_cparams(("parallel", "arbitrary")),
        name="proj",
    )(x, nw, w_sec, w_small)


def _outproj_body(ya_ref, yb_ref, wa_ref, wb_ref, nw_ref, x_ref, o_ref):
    acc = _dot(ya_ref[...].astype(BF16), wa_ref[...]) + _dot(yb_ref[...].astype(BF16), wb_ref[...])
    o_ref[...] = x_ref[...] + _rms(acc, nw_ref[...])


def _outproj(ya, yb, w, nw, x, tm, ya_col=0, yb_col=0):
    m, d = x.shape
    assert m % tm == 0
    return pl.pallas_call(
        _outproj_body,
        grid=(m // tm,),
        in_specs=[
            pl.BlockSpec((tm, SEC), lambda i: (i, ya_col)),
            pl.BlockSpec((tm, SEC), lambda i: (i, yb_col)),
            pl.BlockSpec((SEC, d), lambda i: (0, 0)),
            pl.BlockSpec((SEC, d), lambda i: (1, 0)),
            pl.BlockSpec((1, d), lambda i: (0, 0)),
            pl.BlockSpec((tm, d), lambda i: (i, 0)),
        ],
        out_specs=pl.BlockSpec((tm, d), lambda i: (i, 0)),
        out_shape=jax.ShapeDtypeStruct((m, d), F32),
        compiler_params=_cparams(("parallel",)),
        name="outproj",
    )(ya, yb, w, w, nw, x)


KEY_SEG = LANES // SUBLANES
LOG2E = 1.4426950408889634
SB_BQ = 512
SB_Z2_MAX = 64.0


def _sb_body(bias_ref, q_ref, gate_ref, kp_ref, vt_ref, o_ref, acc_ref, z0_ref, z1_ref, a0_ref, a1_ref,
             *, bq):
    h = pl.program_id(0)
    i = pl.program_id(1)
    nsub = bq // LANES
    qs = (q_ref[...] * (D_B ** -0.5 * LOG2E)).astype(BF16)
    bias2 = bias_ref[h] * LOG2E
    seg = lax.broadcasted_iota(jnp.int32, (SUBLANES, LANES), 0)
    lane = lax.broadcasted_iota(jnp.int32, (SUBLANES, LANES), 1)
    acc_ref[...] = jnp.zeros_like(acc_ref)

    def chain(zt, rel, run):
        part = []
        cs = None
        for r in range(KEY_SEG - 1, -1, -1):
            w = jnp.exp2(jnp.minimum(zt[r * SUBLANES:(r + 1) * SUBLANES], SB_Z2_MAX))
            p = 1.0 / (1.0 + w)
            if rel is not None:
                m = (rel + seg * KEY_SEG + r) < lane
                w = jnp.where(m, w, 0.0)
                p = jnp.where(m, p, 1.0)
            cs = p if cs is None else p * cs
            part.append(w * cs)
        inc = cs
        for d in (1, 2, 4):
            sh = pltpu.roll(inc, SUBLANES - d, axis=0)
            inc = inc * jnp.where(seg + d < SUBLANES, sh, 1.0)
        off = jnp.where(seg < SUBLANES - 1, pltpu.roll(inc, SUBLANES - 1, axis=0), 1.0) * run
        a = [p_ * off for p_ in part[::-1]]
        return jnp.concatenate(a, axis=0), run * inc[0:1]

    def scores(j, z_ref, valid=None):
        j = jnp.maximum(j, 0)
        b = bias2 if valid is None else jnp.where(valid, bias2, -1e30)
        for g in range(nsub):
            kblk = kp_ref[pl.ds(pl.multiple_of(j * bq + g * LANES, LANES), LANES), :]
            z_ref[g] = _dot_nt(kblk, qs) + b

    def weights(z_ref, a_ref, masked, runs):
        runs = list(runs)
        for g in range(nsub - 1, -1, -1):
            for lt in range(nsub):
                cols = slice(lt * LANES, (lt + 1) * LANES)
                rows = slice(g * LANES, (g + 1) * LANES)
                if masked and g > lt:
                    a_ref[rows, cols] = jnp.zeros((LANES, LANES), BF16)
                    continue
                rel = 0 if (masked and g == lt) else None
                a, runs[lt] = chain(z_ref[g, :, cols], rel, runs[lt])
                a_ref[rows, cols] = a.astype(BF16)
        return tuple(runs)

    def values(j, a_ref):
        acc_ref[...] += _dot(vt_ref[jnp.maximum(j, 0)], a_ref[...])

    scores(i, z1_ref)
    runs = weights(z1_ref, a1_ref, True, tuple(jnp.ones((1, LANES), F32) for _ in range(nsub)))
    scores(i - 1, z0_ref)

    def body(t, runs):
        ja = i - 1 - 2 * t
        jb = ja - 1
        values(ja + 1, a1_ref)
        scores(jb, z1_ref, jb >= 0)
        runs = weights(z0_ref, a0_ref, False, runs)
        values(ja, a0_ref)
        scores(jb - 1, z0_ref)
        return weights(z1_ref, a1_ref, False, runs)

    n_pairs = (i + 1) // 2
    lax.fori_loop(0, n_pairs, body, runs)
    values(i - 2 * n_pairs, a1_ref)
    o_ref[...] = (acc_ref[...].T * _silu(gate_ref[...])).astype(o_ref.dtype)


def _sb_prompt(bias, proj, q_sec, z_sec, kp, vt, bq):
    _, t, _ = proj.shape
    assert t % bq == 0 and bq % LANES == 0
    return pl.pallas_call(
        functools.partial(_sb_body, bq=bq),
        grid=(H_B, t // bq),
        in_specs=[
            pl.BlockSpec(memory_space=pltpu.SMEM),
            pl.BlockSpec((None, bq, D_B), lambda h, i: (q_sec, i, h)),
            pl.BlockSpec((None, bq, D_B), lambda h, i: (z_sec, i, h)),
            pl.BlockSpec((None, t, D_B), lambda h, i: (h, 0, 0)),
            pl.BlockSpec((None, t // bq, D_B, bq), lambda h, i: (h, 0, 0, 0)),
        ],
        out_specs=pl.BlockSpec((bq, D_B), lambda h, i: (i, h)),
        out_shape=jax.ShapeDtypeStruct((t, H_B * D_B), BF16),
        scratch_shapes=[pltpu.VMEM((D_B, bq), F32),
                        pltpu.VMEM((bq // LANES, LANES, bq), F32), pltpu.VMEM((bq // LANES, LANES, bq), F32),
                        pltpu.VMEM((bq, bq), BF16), pltpu.VMEM((bq, bq), BF16)],
        compiler_params=_cparams(("parallel", "arbitrary")),
        name="sb_prompt",
    )(bias, proj, proj, kp, vt)


def _sb_key_layouts(k, v, bq):
    t = k.shape[0]
    nb = t // LANES

    def interleave(a):
        a = a.astype(BF16).reshape(nb, SUBLANES, KEY_SEG, H_B, D_B)
        return jnp.transpose(a, (3, 0, 2, 1, 4))

    kp = interleave(k).reshape(H_B, t, D_B)
    vt = jnp.swapaxes(interleave(v).reshape(H_B, t // bq, bq, D_B), 2, 3)
    return kp, vt


def _conv_rows(xp_ref, w_ref, r0, rows, lo):
    acc = None
    for i in range(CONV_W):
        start = SUBLANES - (CONV_W - 1) + i + r0
        term = xp_ref[start:start + rows, lo:lo + LANES] * w_ref[i:i + 1, lo:lo + LANES]
        acc = term if acc is None else acc + term
    return acc


def _split_bf16(a):
    hi = a.astype(BF16)
    return hi, (a - hi.astype(F32)).astype(BF16)


def _dot_split(a, b):
    ah, al = a
    bh, bl = b
    return _dot(ah, bh) + (_dot(ah, bl) + _dot(al, bh))


def _tri_inverse_unit_lower(lmats, n):
    row = lax.broadcasted_iota(jnp.int32, (n, n), 0)
    col = lax.broadcasted_iota(jnp.int32, (n, n), 1)
    eye = jnp.where(row == col, 1.0, 0.0).astype(F32)
    ms = [-l for l in lmats]
    invs = [eye + m for m in ms]
    k = 2
    while k < n:
        splits = [_split_bf16(m) for m in ms]
        ms = [_dot_split(s, s) for s in splits]
        invs = [_dot_split(_split_bf16(iv), _split_bf16(eye + m)) for iv, m in zip(invs, ms)]
        k *= 2
    return invs


def _gdn_body(q_ref, k_ref, v_ref, z_ref, ab_ref, cw_ref, alog_ref, dtb_ref, gn_ref,
              o_ref, s_out_ref, tail_ref, xp_ref, s_ref, *, cpb):
    c = GDN_CHUNK
    rows = cpb * c
    step = pl.program_id(0)

    @pl.when(step == 0)
    def _():
        xp_ref[0:SUBLANES, :] = jnp.zeros((SUBLANES, CONV_DIM_A), F32)
        s_ref[...] = jnp.zeros_like(s_ref)

    xp_ref[SUBLANES:, 0:QK_A] = q_ref[...]
    xp_ref[SUBLANES:, QK_A:2 * QK_A] = k_ref[...]
    xp_ref[SUBLANES:, 2 * QK_A:] = v_ref[...]

    ab = ab_ref[...]
    g_all = -jnp.exp(alog_ref[...]) * _softplus(ab + dtb_ref[...])
    beta_all = _sigmoid(ab)
    ri = lax.broadcasted_iota(jnp.int32, (c, c), 0)
    ci = lax.broadcasted_iota(jnp.int32, (c, c), 1)
    tril = ri >= ci
    strict = ri > ci
    tri_f = jnp.where(tril, 1.0, 0.0).astype(F32)

    items = []
    for cc in range(cpb):
        r0 = cc * c
        gc = _dot_hi(tri_f, g_all[r0:r0 + c])
        gct = gc.T
        beta_c = beta_all[r0:r0 + c]
        for h in range(H_A):
            lo = h * DK_A
            qh = _silu(_conv_rows(xp_ref, cw_ref, r0, c, lo))
            kh = _silu(_conv_rows(xp_ref, cw_ref, r0, c, QK_A + lo))
            vh = _silu(_conv_rows(xp_ref, cw_ref, r0, c, 2 * QK_A + lo))
            qh = qh * lax.rsqrt(jnp.sum(qh * qh, axis=-1, keepdims=True) + 1e-6) * (DK_A ** -0.5)
            kh = kh * lax.rsqrt(jnp.sum(kh * kh, axis=-1, keepdims=True) + 1e-6)
            gcol = gc[:, h:h + 1]
            grow = gct[h:h + 1, :]
            beta = beta_c[:, H_A + h:H_A + h + 1]
            decay = jnp.where(tril, jnp.exp(jnp.where(tril, gcol - grow, 0.0)), 0.0)
            kb = kh.astype(BF16)
            lmat = jnp.where(strict, beta * _dot_nt(kb, kb) * decay, 0.0)
            egc = jnp.exp(gcol)
            glast = gcol[c - 1:c]
            items.append(dict(
                r0=r0, h=h, lmat=lmat,
                rhs=jnp.concatenate([vh * beta, kh * (beta * egc)], axis=-1),
                qk=jnp.where(tril, _dot_nt(qh.astype(BF16), kb) * decay, 0.0).astype(BF16),
                qg=(qh * egc).astype(BF16),
                kdec_t=(kh * jnp.exp(glast - gcol)).T.astype(BF16),
                sdecay=jnp.exp(glast)))
    invs = _tri_inverse_unit_lower([it["lmat"] for it in items], c)
    sols = [_dot_split(_split_bf16(iv), _split_bf16(it["rhs"])) for iv, it in zip(invs, items)]

    for it, sol in zip(items, sols):
        r0, h = it["r0"], it["h"]
        lo = h * DV_A
        s_old = s_ref[h]
        sb = s_old.astype(BF16)
        v_new = sol[:, :DV_A] - _dot(sol[:, DV_A:].astype(BF16), sb)
        vnb = v_new.astype(BF16)
        o = _dot(it["qg"], sb) + _dot(it["qk"], vnb)
        s_ref[h] = s_old * it["sdecay"] + _dot(it["kdec_t"], vnb)
        o = _rms(o, gn_ref[...]) * _silu(z_ref[r0:r0 + c, lo:lo + DV_A])
        o_ref[r0:r0 + c, lo:lo + DV_A] = o.astype(o_ref.dtype)

    xp_ref[0:SUBLANES, :] = xp_ref[rows:rows + SUBLANES, :]

    @pl.when(step == pl.num_programs(0) - 1)
    def _():
        s_out_ref[...] = s_ref[...]
        tail_ref[...] = xp_ref[0:SUBLANES, :]


def _gdn_prompt(proj, ab, conv_w, a_log_row, dt_bias_row, g_norm, cpb):
    _, t, _ = proj.shape
    rows = cpb * GDN_CHUNK
    assert t % rows == 0
    sec = lambda s: pl.BlockSpec((None, rows, SEC), lambda i: (s, i, 0))
    full = lambda shape: pl.BlockSpec(shape, lambda i: (0,) * len(shape))
    return pl.pallas_call(
        functools.partial(_gdn_body, cpb=cpb),
        grid=(t // rows,),
        in_specs=[sec(0), sec(1), sec(2), sec(3),
                  pl.BlockSpec((rows, LANES), lambda i: (i, 0)),
                  full((CONV_W, CONV_DIM_A)), full((1, LANES)), full((1, LANES)), full((1, DV_A))],
        out_specs=[pl.BlockSpec((rows, V_A), lambda i: (i, 0)),
                   full((H_A, DK_A, DV_A)), full((SUBLANES, CONV_DIM_A))],
        out_shape=[jax.ShapeDtypeStruct((t, V_A), BF16),
                   jax.ShapeDtypeStruct((H_A, DK_A, DV_A), F32),
                   jax.ShapeDtypeStruct((SUBLANES, CONV_DIM_A), F32)],
        scratch_shapes=[pltpu.VMEM((rows + SUBLANES, CONV_DIM_A), F32),
                        pltpu.VMEM((H_A, DK_A, DV_A), F32)],
        compiler_params=_cparams(("arbitrary",)),
        name="gdn_prompt",
    )(proj, proj, proj, proj, ab, conv_w, a_log_row, dt_bias_row, g_norm)


def _ssd_body(z0_ref, z1_ref, x0_ref, x1_ref, bc_ref, dt_ref, cw_ref, cb_ref, alog_ref, dtb_ref,
              dskip_ref, gn_ref, exp_ref, o_ref, s_out_ref, tail_ref, xp_ref, s_ref, *, cpb):
    c = SSD_CHUNK
    rows = cpb * c
    step = pl.program_id(0)
    gw = D_INNER_C // N_GROUPS_C
    n_bc = N_GROUPS_C * D_STATE_C

    @pl.when(step == 0)
    def _():
        xp_ref[0:SUBLANES, :] = jnp.zeros((SUBLANES, CONV_DIM_C), F32)
        s_ref[...] = jnp.zeros_like(s_ref)

    xp_ref[SUBLANES:, 0:SEC] = x0_ref[...]
    xp_ref[SUBLANES:, SEC:2 * SEC] = x1_ref[...]
    xp_ref[SUBLANES:, 2 * SEC:] = bc_ref[...]

    dt_all = _softplus(dt_ref[...] + dtb_ref[...])
    da_all = dt_all * (-jnp.exp(alog_ref[...]))
    ri = lax.broadcasted_iota(jnp.int32, (c, c), 0)
    ci = lax.broadcasted_iota(jnp.int32, (c, c), 1)
    tril = ri >= ci
    tri_f = jnp.where(tril, 1.0, 0.0).astype(F32)
    lane = lax.broadcasted_iota(jnp.int32, (1, LANES), 1)
    left = lane < HEAD_DIM_C
    rowi = lax.broadcasted_iota(jnp.int32, (LANES, 1), 0)
    top = rowi < HEAD_DIM_C
    expand = exp_ref[...]

    for cc in range(cpb):
        r0 = cc * c
        dt = dt_all[r0:r0 + c]
        acum = _dot_hi(tri_f, da_all[r0:r0 + c])
        acum_t = acum.T
        dt_x = _select_dot(dt, expand)
        ac_x = _select_dot(acum, expand)
        for g in range(N_GROUPS_C):
            bm = _silu(_conv_rows(xp_ref, cw_ref, r0, c, D_INNER_C + g * D_STATE_C)
                       + cb_ref[:, D_INNER_C + g * D_STATE_C:D_INNER_C + (g + 1) * D_STATE_C])
            cm = _silu(_conv_rows(xp_ref, cw_ref, r0, c, D_INNER_C + n_bc + g * D_STATE_C)
                       + cb_ref[:, D_INNER_C + n_bc + g * D_STATE_C:D_INNER_C + n_bc + (g + 1) * D_STATE_C])
            bmb = bm.astype(BF16)
            cmb = cm.astype(BF16)
            cb = _dot_nt(cmb, bmb)
            ys = []
            for pr in range(gw // LANES):
                lo = g * gw + pr * LANES
                h0 = lo // HEAD_DIM_C
                xs = _silu(_conv_rows(xp_ref, cw_ref, r0, c, lo) + cb_ref[:, lo:lo + LANES])
                xdt = xs * dt_x[:, lo:lo + LANES]
                acx = ac_x[:, lo:lo + LANES]
                y = None
                for hh in range(2):
                    hd = h0 + hh
                    seg = acum[:, hd:hd + 1] - acum_t[hd:hd + 1, :]
                    lm = jnp.where(tril, jnp.exp(jnp.where(tril, seg, 0.0)), 0.0)
                    xm = jnp.where(left if hh == 0 else jnp.logical_not(left), xdt, 0.0)
                    term = _dot((cb * lm).astype(BF16), xm.astype(BF16))
                    y = term if y is None else y + term
                s_old = s_ref[lo:lo + LANES, :]
                y = y + _dot_nt(cmb, s_old.astype(BF16)) * jnp.exp(acx)
                dte = jnp.exp(acx[c - 1:c] - acx)
                st = _dot((xdt * dte).T.astype(BF16), bmb)
                al0 = acum[c - 1:c, h0:h0 + 1]
                al1 = acum[c - 1:c, h0 + 1:h0 + 2]
                s_ref[lo:lo + LANES, :] = s_old * jnp.exp(jnp.where(top, al0, al1)) + st
                y = y + xs * dskip_ref[:, lo:lo + LANES]
                zz = (z0_ref if lo < SEC else z1_ref)[r0:r0 + c, (lo % SEC):(lo % SEC) + LANES]
                ys.append(y * _silu(zz))
            yg = jnp.concatenate(ys, axis=-1)
            yg = _rms(yg, gn_ref[:, g * gw:(g + 1) * gw])
            o_ref[r0:r0 + c, g * gw:(g + 1) * gw] = yg.astype(o_ref.dtype)

    xp_ref[0:SUBLANES, :] = xp_ref[rows:rows + SUBLANES, :]

    @pl.when(step == pl.num_programs(0) - 1)
    def _():
        s_out_ref[...] = s_ref[...]
        tail_ref[...] = xp_ref[0:SUBLANES, :]


def _ssd_prompt(proj, dt_raw, conv_w, conv_b, a_log_row, dt_bias_row, dskip_x, g_norm, expand, cpb):
    _, t, _ = proj.shape
    rows = cpb * SSD_CHUNK
    assert t % rows == 0
    sec = lambda s: pl.BlockSpec((None, rows, SEC), lambda i: (s, i, 0))
    full = lambda shape: pl.BlockSpec(shape, lambda i: (0,) * len(shape))
    return pl.pallas_call(
        functools.partial(_ssd_body, cpb=cpb),
        grid=(t // rows,),
        in_specs=[sec(0), sec(1), sec(2), sec(3), sec(4),
                  pl.BlockSpec((rows, LANES), lambda i: (i, 0)),
                  full((CONV_W, CONV_DIM_C)), full((1, CONV_DIM_C)), full((1, LANES)), full((1, LANES)),
                  full((1, D_INNER_C)), full((1, D_INNER_C)), full((LANES, D_INNER_C))],
        out_specs=[pl.BlockSpec((rows, D_INNER_C), lambda i: (i, 0)),
                   full((D_INNER_C, D_STATE_C)), full((SUBLANES, CONV_DIM_C))],
        out_shape=[jax.ShapeDtypeStruct((t, D_INNER_C), BF16),
                   jax.ShapeDtypeStruct((D_INNER_C, D_STATE_C), F32),
                   jax.ShapeDtypeStruct((SUBLANES, CONV_DIM_C), F32)],
        scratch_shapes=[pltpu.VMEM((rows + SUBLANES, CONV_DIM_C), F32),
                        pltpu.VMEM((D_INNER_C, D_STATE_C), F32)],
        compiler_params=_cparams(("arbitrary",)),
        name="ssd_prompt",
    )(proj, proj, proj, proj, proj, dt_raw, conv_w, conv_b, a_log_row, dt_bias_row, dskip_x, g_norm, expand)


def _head_expand():
    r = jnp.arange(LANES)[:, None]
    cidx = jnp.arange(D_INNER_C)[None, :]
    return (cidx // HEAD_DIM_C == r).astype(BF16)


def _row_to_col(row, eye):
    return jnp.sum(jnp.where(eye, row, 0.0), axis=1, keepdims=True)


def _col_to_row(col, eye):
    return jnp.sum(jnp.where(eye, col, 0.0), axis=0, keepdims=True)


def _eye(n):
    return lax.broadcasted_iota(jnp.int32, (n, n), 0) == lax.broadcasted_iota(jnp.int32, (n, n), 1)


def _conv_step(x_row, buf_ref, w_ref, lo, hi):
    acc = x_row * w_ref[CONV_W - 1:CONV_W, lo:hi]
    for i in range(CONV_W - 1):
        acc = acc + buf_ref[i:i + 1, lo:hi] * w_ref[i:i + 1, lo:hi]
    return acc


def _gdn_step_body(q_ref, k_ref, v_ref, z_ref, ab_ref, buf_ref, s_ref, cw_ref, alog_ref, dtb_ref, gn_ref,
                   o_ref, s_out_ref, buf_out_ref):
    eye = _eye(LANES)
    ab = ab_ref[...]
    eg_all = jnp.exp(-jnp.exp(alog_ref[...]) * _softplus(ab + dtb_ref[...]))
    beta_all = _sigmoid(ab)
    buf_out_ref[0:CONV_W - 2, :] = buf_ref[1:CONV_W - 1, :]
    pre = (q_ref, k_ref, v_ref)
    for n in range(3):
        buf_out_ref[CONV_W - 2:CONV_W - 1, n * SEC:(n + 1) * SEC] = pre[n][...]
    for h in range(H_A):
        lo = h * DK_A
        q = _silu(_conv_step(q_ref[:, lo:lo + DK_A], buf_ref, cw_ref, lo, lo + DK_A))
        k = _silu(_conv_step(k_ref[:, lo:lo + DK_A], buf_ref, cw_ref, QK_A + lo, QK_A + lo + DK_A))
        v = _silu(_conv_step(v_ref[:, lo:lo + DV_A], buf_ref, cw_ref, 2 * QK_A + lo, 2 * QK_A + lo + DV_A))
        q = q * lax.rsqrt(jnp.sum(q * q, axis=-1, keepdims=True) + 1e-6) * (DK_A ** -0.5)
        k = k * lax.rsqrt(jnp.sum(k * k, axis=-1, keepdims=True) + 1e-6)
        eg = eg_all[:, h:h + 1]
        beta = beta_all[:, H_A + h:H_A + h + 1]
        s0 = s_ref[h]
        kc = _row_to_col(k, eye)
        qc = _row_to_col(q, eye)
        sk = jnp.sum(s0 * kc, axis=0, keepdims=True)
        sq = jnp.sum(s0 * qc, axis=0, keepdims=True)
        v_new = beta * (v - eg * sk)
        o = eg * sq + jnp.sum(q * k, axis=-1, keepdims=True) * v_new
        s_out_ref[h] = s0 * eg + kc * v_new
        o_ref[:, lo:lo + DV_A] = _rms(o, gn_ref[...]) * _silu(z_ref[:, lo:lo + DV_A])


def _gdn_step(proj, ab, conv_buf, state, conv_w, a_log_row, dt_bias_row, g_norm):
    b = ab.shape[0]
    sec = lambda s: pl.BlockSpec((None, None, 1, SEC), lambda i: (s, i, 0, 0))
    full = lambda shape: pl.BlockSpec(shape, lambda i: (0,) * len(shape))
    return pl.pallas_call(
        _gdn_step_body,
        grid=(b,),
        in_specs=[sec(0), sec(1), sec(2), sec(3),
                  pl.BlockSpec((None, 1, LANES), lambda i: (i, 0, 0)),
                  pl.BlockSpec((None, CONV_W - 1, CONV_DIM_A), lambda i: (i, 0, 0)),
                  pl.BlockSpec((None, H_A, DK_A, DV_A), lambda i: (i, 0, 0, 0)),
                  full((CONV_W, CONV_DIM_A)), full((1, LANES)), full((1, LANES)), full((1, DV_A))],
        out_specs=[pl.BlockSpec((None, 1, V_A), lambda i: (i, 0, 0)),
                   pl.BlockSpec((None, H_A, DK_A, DV_A), lambda i: (i, 0, 0, 0)),
                   pl.BlockSpec((None, CONV_W - 1, CONV_DIM_A), lambda i: (i, 0, 0))],
        out_shape=[jax.ShapeDtypeStruct((b, 1, V_A), F32),
                   jax.ShapeDtypeStruct((b, H_A, DK_A, DV_A), F32),
                   jax.ShapeDtypeStruct((b, CONV_W - 1, CONV_DIM_A), F32)],
        compiler_params=_cparams(("parallel",)),
        name="gdn_step",
    )(proj, proj, proj, proj, ab, conv_buf, state, conv_w, a_log_row, dt_bias_row, g_norm)


def _ssd_step_body(z0_ref, z1_ref, x0_ref, x1_ref, bc_ref, dt_ref, buf_ref, s_ref, cw_ref, cb_ref, alog_ref,
                   dtb_ref, dskip_ref, gn_ref, o_ref, s_out_ref, buf_out_ref):
    eye = _eye(LANES)
    gw = D_INNER_C // N_GROUPS_C
    n_bc = N_GROUPS_C * D_STATE_C
    top = lax.broadcasted_iota(jnp.int32, (LANES, 1), 0) < HEAD_DIM_C
    dt_all = _softplus(dt_ref[...] + dtb_ref[...])
    dec_all = jnp.exp(dt_all * (-jnp.exp(alog_ref[...])))
    buf_out_ref[0:CONV_W - 2, :] = buf_ref[1:CONV_W - 1, :]
    pre = (x0_ref, x1_ref, bc_ref)
    for n in range(3):
        buf_out_ref[CONV_W - 2:CONV_W - 1, n * SEC:(n + 1) * SEC] = pre[n][...]

    def conv(ref, lo_in, lo):
        return _silu(_conv_step(ref[:, lo_in:lo_in + LANES], buf_ref, cw_ref, lo, lo + LANES)
                     + cb_ref[:, lo:lo + LANES])

    for g in range(N_GROUPS_C):
        bm = conv(bc_ref, g * D_STATE_C, D_INNER_C + g * D_STATE_C)
        cm = conv(bc_ref, n_bc + g * D_STATE_C, D_INNER_C + n_bc + g * D_STATE_C)
        ys = []
        for pr in range(gw // LANES):
            lo = g * gw + pr * LANES
            h0 = lo // HEAD_DIM_C
            xs = conv(x0_ref if lo < SEC else x1_ref, lo % SEC, lo)
            xc = _row_to_col(xs, eye)
            dt_c = jnp.where(top, dt_all[:, h0:h0 + 1], dt_all[:, h0 + 1:h0 + 2])
            dec_c = jnp.where(top, dec_all[:, h0:h0 + 1], dec_all[:, h0 + 1:h0 + 2])
            h_new = s_ref[lo:lo + LANES, :] * dec_c + (xc * dt_c) * bm
            s_out_ref[lo:lo + LANES, :] = h_new
            y = _col_to_row(jnp.sum(h_new * cm, axis=1, keepdims=True), eye)
            y = y + xs * dskip_ref[:, lo:lo + LANES]
            zz = (z0_ref if lo < SEC else z1_ref)[:, (lo % SEC):(lo % SEC) + LANES]
            ys.append(y * _silu(zz))
        yg = jnp.concatenate(ys, axis=-1)
        o_ref[:, g * gw:(g + 1) * gw] = _rms(yg, gn_ref[:, g * gw:(g + 1) * gw])


def _ssd_step(proj, dt_raw, conv_buf, state, conv_w, conv_b, a_log_row, dt_bias_row, dskip_x, g_norm):
    b = dt_raw.shape[0]
    sec = lambda s: pl.BlockSpec((None, None, 1, SEC), lambda i: (s, i, 0, 0))
    full = lambda shape: pl.BlockSpec(shape, lambda i: (0,) * len(shape))
    return pl.pallas_call(
        _ssd_step_body,
        grid=(b,),
        in_specs=[sec(0), sec(1), sec(2), sec(3), sec(4),
                  pl.BlockSpec((None, 1, LANES), lambda i: (i, 0, 0)),
                  pl.BlockSpec((None, CONV_W - 1, CONV_DIM_C), lambda i: (i, 0, 0)),
                  pl.BlockSpec((None, D_INNER_C, D_STATE_C), lambda i: (i, 0, 0)),
                  full((CONV_W, CONV_DIM_C)), full((1, CONV_DIM_C)), full((1, LANES)), full((1, LANES)),
                  full((1, D_INNER_C)), full((1, D_INNER_C))],
        out_specs=[pl.BlockSpec((None, 1, D_INNER_C), lambda i: (i, 0, 0)),
                   pl.BlockSpec((None, D_INNER_C, D_STATE_C), lambda i: (i, 0, 0)),
                   pl.BlockSpec((None, CONV_W - 1, CONV_DIM_C), lambda i: (i, 0, 0))],
        out_shape=[jax.ShapeDtypeStruct((b, 1, D_INNER_C), F32),
                   jax.ShapeDtypeStruct((b, D_INNER_C, D_STATE_C), F32),
                   jax.ShapeDtypeStruct((b, CONV_W - 1, CONV_DIM_C), F32)],
        compiler_params=_cparams(("parallel",)),
        name="ssd_step",
    )(proj, proj, proj, proj, proj, dt_raw, conv_buf, state, conv_w, conv_b, a_log_row, dt_bias_row,
      dskip_x, g_norm)


SB_PAGES_PER_STEP = 8


def _sb_paged_body(pt_ref, q_ref, z_ref, bias_ref, *refs, ppb):
    del pt_ref
    k_refs = refs[:ppb]
    v_refs = refs[ppb:2 * ppb]
    o_ref, acc_ref, run_ref = refs[2 * ppb:]
    p = pl.program_id(1)
    grp = PAGE_SIZE // KEY_SEG
    rows = KEY_SEG * H_B

    @pl.when(p == 0)
    def _():
        acc_ref[...] = jnp.zeros_like(acc_ref)
        run_ref[...] = jnp.zeros_like(run_ref)

    sub = lax.broadcasted_iota(jnp.int32, (SUBLANES, LANES), 0)
    lane = lax.broadcasted_iota(jnp.int32, (SUBLANES, LANES), 1)
    own = (lane % H_B) == sub
    ri = lax.broadcasted_iota(jnp.int32, (rows, 2 * rows), 0)
    ci = lax.broadcasted_iota(jnp.int32, (rows, 2 * rows), 1)
    same_head = (ri % H_B) == (ci % H_B)
    tmat = jnp.where(same_head & ((ci >= rows) | (ri // H_B >= ci // H_B)), 1.0, 0.0).astype(BF16)
    qb = q_ref[...].astype(BF16)
    bias = bias_ref[...]
    pages = range(ppb)
    k2 = [k_refs[r][...].reshape(PAGE_SIZE * H_B, D_B).astype(BF16) for r in pages]
    zs = []
    for r in pages:
        zrows = []
        for g in range(grp):
            zt = _dot_nt(qb, k2[r][g * rows:(g + 1) * rows])
            zrows.append(jnp.sum(jnp.where(own, zt, 0.0), axis=0, keepdims=True))
        zs.append(jnp.concatenate(zrows, axis=0) * (D_B ** -0.5) + bias)
    sps = [_split_bf16(_softplus(z)) for z in zs]
    css = [_dot(hi, tmat) + _dot(lo, tmat) for hi, lo in sps]
    offs = []
    tots = []
    for cs in css:
        inc = cs[:, rows:]
        for d in (1, 2, 4):
            sh = pltpu.roll(inc, SUBLANES - d, axis=0)
            inc = inc + jnp.where(sub + d < SUBLANES, sh, 0.0)
        offs.append(jnp.where(sub < SUBLANES - 1, pltpu.roll(inc, SUBLANES - 1, axis=0), 0.0))
        tots.append(inc[0:1])
    run_after = run_ref[...]
    acc = acc_ref[...]
    for r in pages:
        a = jnp.exp(zs[r] - (css[r][:, :rows] + (offs[r] + run_after)))
        lhs = jnp.concatenate([jnp.where(own, a[g:g + 1], 0.0) for g in range(grp)], axis=1)
        v2 = v_refs[r][...].reshape(PAGE_SIZE * H_B, D_B).astype(BF16)
        acc = acc + _dot(lhs.astype(BF16), v2)
        run_after = run_after + tots[r]
    acc_ref[...] = acc
    run_ref[...] = run_after

    @pl.when(p == pl.num_programs(1) - 1)
    def _():
        o_ref[...] = acc * _silu(z_ref[...])


def _sb_paged(page_table, q, zgate, bias_row, cache_k, cache_v, layer):
    b, npg = page_table.shape
    ppb = SB_PAGES_PER_STEP
    assert npg % ppb == 0

    def page_spec(r):
        return pl.BlockSpec((None, None, PAGE_SIZE, H_B, D_B),
                            lambda i, p, pt: (layer, pt[i, npg - 1 - (p * ppb + r)], 0, 0, 0))

    row = pl.BlockSpec((None, H_B, D_B), lambda i, p, pt: (i, 0, 0))
    grid_spec = pltpu.PrefetchScalarGridSpec(
        num_scalar_prefetch=1,
        grid=(b, npg // ppb),
        in_specs=[row, row, pl.BlockSpec((1, LANES), lambda i, p, pt: (0, 0))]
        + [page_spec(r) for r in range(ppb)] * 2,
        out_specs=row,
        scratch_shapes=[pltpu.VMEM((H_B, D_B), F32), pltpu.VMEM((1, LANES), F32)],
    )
    return pl.pallas_call(
        functools.partial(_sb_paged_body, ppb=ppb),
        grid_spec=grid_spec,
        out_shape=jax.ShapeDtypeStruct((b, H_B, D_B), F32),
        compiler_params=_cparams(("parallel", "arbitrary")),
        name="sb_paged",
    )(page_table, q, zgate, bias_row, *([cache_k] * ppb), *([cache_v] * ppb))


def _pad_lanes(v):
    return jnp.pad(v.astype(F32), (0, LANES - v.shape[0])).reshape(1, LANES)


def _even_weights(w_in):
    o_ab = CONV_DIM_A
    o_z = o_ab + 2 * H_A
    w_main = jnp.concatenate([w_in[:, :o_ab], w_in[:, o_z:]], axis=1).astype(BF16)
    w_sec = jnp.transpose(w_main.reshape(D_MODEL, 8, SEC), (1, 0, 2))
    w_small = jnp.pad(w_in[:, o_ab:o_z], ((0, 0), (0, LANES - 2 * H_A))).astype(BF16)
    return w_sec, w_small


def _odd_weights(w_in):
    n_main = D_INNER_C + CONV_DIM_C
    w_sec = jnp.transpose(w_in[:, :n_main].astype(BF16).reshape(D_MODEL, 5, SEC), (1, 0, 2))
    w_small = jnp.pad(w_in[:, n_main:], ((0, 0), (0, LANES - H_C))).astype(BF16)
    return w_sec, w_small


def kernel(x_prompt, x_sample, cache_sb_k, cache_sb_v, state_gdn, state_gdn_conv, state_ssd, state_ssd_conv, page_table, even_norm_pre, even_norm_post, even_w_in, gdn_conv_w, gdn_a_log, gdn_dt_bias, gdn_norm_w, sb_bias, even_w_out, odd_norm_pre, odd_norm_post, odd_w_in, ssd_conv_w, ssd_conv_b, ssd_a_log, ssd_dt_bias, ssd_d, ssd_norm_w, odd_w_out):
    t = x_prompt.shape[1]
    bs = x_sample.shape[0]
    xp = x_prompt[0]
    xs = x_sample[:, 0]
    tail = SUBLANES - (CONV_W - 1)
    tm_p = 1024
    tm_s = bs

    w_sec, w_small = _even_weights(even_w_in[0])
    w_out_e = even_w_out[0].astype(BF16)
    a_log_e = _pad_lanes(gdn_a_log[0])
    dt_bias_e = _pad_lanes(gdn_dt_bias[0])
    proj, ab, k_b, v_b = _proj(xp, even_norm_pre, w_sec, w_small, tm_p, own=(5, 6))
    sb_k_prompt = k_b.reshape(1, 1, t, H_B, D_B)
    sb_v_prompt = v_b.reshape(1, 1, t, H_B, D_B)
    kp, vt = _sb_key_layouts(k_b, v_b, SB_BQ)
    o_b = _sb_prompt(sb_bias[0], proj, 4, 5, kp, vt, SB_BQ)
    o_a, gdn_s, gdn_tail = _gdn_prompt(proj, ab, gdn_conv_w[0], a_log_e, dt_bias_e, gdn_norm_w, 4)
    y1 = _outproj(o_a, o_b, w_out_e, even_norm_post, xp, tm_p)
    proj_s, ab_s, k_b_s, v_b_s = _proj(xs, even_norm_pre, w_sec, w_small, tm_s, own=(5, 6))
    sb_k_sample = k_b_s.reshape(1, bs, 1, H_B, D_B)
    sb_v_sample = v_b_s.reshape(1, bs, 1, H_B, D_B)
    o_b_s = _sb_paged(page_table, proj_s[4].reshape(bs, H_B, D_B), proj_s[5].reshape(bs, H_B, D_B),
                      jnp.tile(sb_bias[0].astype(F32), LANES // H_B).reshape(1, LANES),
                      cache_sb_k, cache_sb_v, 0)
    o_a_s, gdn_s_s, gdn_conv_s = _gdn_step(proj_s.reshape(6, bs, 1, SEC), ab_s.reshape(bs, 1, LANES),
                                           state_gdn_conv[0], state_gdn[0], gdn_conv_w[0], a_log_e,
                                           dt_bias_e, gdn_norm_w)
    y1_s = _outproj(o_a_s.reshape(bs, V_A), o_b_s.reshape(bs, W_B), w_out_e, even_norm_post, xs, tm_s)

    w_sec_o, w_small_o = _odd_weights(odd_w_in[0])
    w_out_o = odd_w_out[0].astype(BF16)
    a_log_o = _pad_lanes(ssd_a_log[0])
    dt_bias_o = _pad_lanes(ssd_dt_bias[0])
    dskip_x = jnp.repeat(ssd_d[0].astype(F32), HEAD_DIM_C).reshape(1, D_INNER_C)
    proj_o, dt_raw = _proj(y1, odd_norm_pre, w_sec_o, w_small_o, tm_p)
    y_ssd, ssd_s, ssd_tail = _ssd_prompt(proj_o, dt_raw, ssd_conv_w[0], ssd_conv_b, a_log_o, dt_bias_o,
                                         dskip_x, ssd_norm_w, _head_expand(), 2)
    y2 = _outproj(y_ssd, y_ssd, w_out_o, odd_norm_post, y1, tm_p, 0, 1)
    proj_os, dt_raw_s = _proj(y1_s, odd_norm_pre, w_sec_o, w_small_o, tm_s)
    y_ssd_s, ssd_s_s, ssd_conv_s = _ssd_step(proj_os.reshape(5, bs, 1, SEC), dt_raw_s.reshape(bs, 1, LANES),
                                             state_ssd_conv[0], state_ssd[0].reshape(bs, D_INNER_C, D_STATE_C),
                                             ssd_conv_w[0], ssd_conv_b, a_log_o, dt_bias_o, dskip_x, ssd_norm_w)
    y_ssd_s = y_ssd_s.reshape(bs, D_INNER_C)
    y2_s = _outproj(y_ssd_s, y_ssd_s, w_out_o, odd_norm_post, y1_s, tm_s, 0, 1)

    return (y2[None], y2_s[:, None], sb_k_prompt, sb_v_prompt, sb_k_sample, sb_v_sample,
            gdn_s[None, None], gdn_s_s[None], gdn_tail[tail:][None, None], gdn_conv_s[None],
            ssd_s.reshape(1, 1, H_C, HEAD_DIM_C, D_STATE_C),
            ssd_s_s.reshape(1, bs, H_C, HEAD_DIM_C, D_STATE_C),
            ssd_tail[tail:][None, None], ssd_conv_s[None])
```

```python
import functools

import jax
import jax.numpy as jnp
from jax import lax
from jax.experimental import pallas as pl
from jax.experimental.pallas import tpu as pltpu

F32 = jnp.float32
BF16 = jnp.bfloat16
HIGHEST = lax.Precision.HIGHEST

LANES = 128
SUBLANES = 8
VMEM_LIMIT_BYTES = 48 * 1024 * 1024

D_MODEL = 1024
H_A = 8
DK_A = 128
DV_A = 128
H_B = 8
D_B = 128
CONV_W = 4
GDN_CHUNK = 64
QK_A = H_A * DK_A
V_A = H_A * DV_A
W_B = H_B * D_B
CONV_DIM_A = 2 * QK_A + V_A
D_INNER_C = 2 * D_MODEL
HEAD_DIM_C = 64
H_C = D_INNER_C // HEAD_DIM_C
N_GROUPS_C = 4
D_STATE_C = 128
SSD_CHUNK = 64
CONV_DIM_C = D_INNER_C + 2 * N_GROUPS_C * D_STATE_C
PAGE_SIZE = 128
RMS_EPS = 1e-6
SEC = 1024


def _cparams(sem):
    return pltpu.CompilerParams(dimension_semantics=sem, vmem_limit_bytes=VMEM_LIMIT_BYTES)


def _dot(a, b):
    return jnp.dot(a, b, preferred_element_type=F32)


def _dot_nt(a, b):
    return lax.dot_general(a, b, (((1,), (1,)), ((), ())), preferred_element_type=F32)


def _dot_hi(a, b):
    return jnp.dot(a, b, preferred_element_type=F32, precision=HIGHEST)


def _split3_bf16(x):
    hi = x.astype(BF16)
    r1 = x - hi.astype(F32)
    mid = r1.astype(BF16)
    return hi, mid, (r1 - mid.astype(F32)).astype(BF16)


def _select_dot(x, sel):
    hi, mid, lo = _split3_bf16(x)
    return _dot(hi, sel) + (_dot(mid, sel) + _dot(lo, sel))


def _sigmoid(x):
    return 1.0 / (1.0 + jnp.exp(-x))


def _silu(x):
    return x * _sigmoid(x)


def _softplus(x):
    return jnp.maximum(x, 0.0) + jnp.log1p(jnp.exp(-jnp.abs(x)))


def _rms(x, w):
    return x * lax.rsqrt(jnp.mean(x * x, axis=-1, keepdims=True) + RMS_EPS) * w


def _proj_body(x_ref, nw_ref, w_ref, ws_ref, o_ref, os_ref, *rest, own):
    own_refs, h_ref = rest[:-1], rest[-1]
    j = pl.program_id(1)

    @pl.when(j == 0)
    def _():
        hb = _rms(x_ref[...], nw_ref[...]).astype(BF16)
        h_ref[...] = hb
        os_ref[...] = _dot(hb, ws_ref[...])

    res = _dot(h_ref[...], w_ref[...])
    stacked = None
    for sec, ref in zip(own, own_refs):
        @pl.when(j == sec)
        def _(ref=ref):
            ref[...] = res

        stacked = (j != sec) if stacked is None else stacked & (j != sec)
    if stacked is None:
        o_ref[...] = res
    else:
        @pl.when(stacked)
        def _():
            o_ref[...] = res


def _proj(x, nw, w_sec, w_small, tm, own=()):
    m, d = x.shape
    s = w_sec.shape[0]
    assert m % tm == 0

    def stacked_index(j):
        return j - sum((j >= e).astype(jnp.int32) for e in own) if own else j

    own_spec = pl.BlockSpec((tm, SEC), lambda i, j: (i, 0))
    return pl.pallas_call(
        functools.partial(_proj_body, own=own),
        grid=(m // tm, s),
        in_specs=[
            pl.BlockSpec((tm, d), lambda i, j: (i, 0)),
            pl.BlockSpec((1, d), lambda i, j: (0, 0)),
            pl.BlockSpec((None, d, SEC), lambda i, j: (j, 0, 0)),
            pl.BlockSpec((d, LANES), lambda i, j: (0, 0)),
        ],
        out_specs=[
            pl.BlockSpec((None, tm, SEC), lambda i, j: (stacked_index(j), i, 0)),
            pl.BlockSpec((tm, LANES), lambda i, j: (i, 0)),
        ] + [own_spec] * len(own),
        out_shape=[
            jax.ShapeDtypeStruct((s - len(own), m, SEC), F32),
            jax.ShapeDtypeStruct((m, LANES), F32),
        ] + [jax.ShapeDtypeStruct((m, SEC), F32)] * len(own),
        scratch_shapes=[pltpu.VMEM((tm, d), BF16)],
        compiler_params=_cparams(("parallel", "arbitrary")),
        name="proj",
    )(x, nw, w_sec, w_small)


def _outproj_body(ya_ref, yb_ref, wa_ref, wb_ref, nw_ref, x_ref, o_ref):
    acc = _dot(ya_ref[...].astype(BF16), wa_ref[...]) + _dot(yb_ref[...].astype(BF16), wb_ref[...])
    o_ref[...] = x_ref[...] + _rms(acc, nw_ref[...])


def _outproj(ya, yb, w, nw, x, tm, ya_col=0, yb_col=0):
    m, d = x.shape
    assert m % tm == 0
    return pl.pallas_call(
        _outproj_body,
        grid=(m // tm,),
        in_specs=[
            pl.BlockSpec((tm, SEC), lambda i: (i, ya_col)),
            pl.BlockSpec((tm, SEC), lambda i: (i, yb_col)),
            pl.BlockSpec((SEC, d), lambda i: (0, 0)),
            pl.BlockSpec((SEC, d), lambda i: (1, 0)),
            pl.BlockSpec((1, d), lambda i: (0, 0)),
            pl.BlockSpec((tm, d), lambda i: (i, 0)),
        ],
        out_specs=pl.BlockSpec((tm, d), lambda i: (i, 0)),
        out_shape=jax.ShapeDtypeStruct((m, d), F32),
        compiler_params=_cparams(("parallel",)),
        name="outproj",
    )(ya, yb, w, w, nw, x)


KEY_SEG = LANES // SUBLANES
LOG2E = 1.4426950408889634
SB_BQ = 512
SB_Z2_MAX = 64.0


def _sb_body(bias_ref, q_ref, gate_ref, kp_ref, vt_ref, o_ref, acc_ref, z0_ref, z1_ref, a0_ref, a1_ref,
             *, bq):
    h = pl.program_id(0)
    i = pl.program_id(1)
    nsub = bq // LANES
    qt = (q_ref[...] * (D_B ** -0.5 * LOG2E)).T.astype(BF16)
    bias2 = bias_ref[h] * LOG2E
    seg = lax.broadcasted_iota(jnp.int32, (SUBLANES, LANES), 0)
    lane = lax.broadcasted_iota(jnp.int32, (SUBLANES, LANES), 1)
    acc_ref[...] = jnp.zeros_like(acc_ref)

    def chain(zt, rel, run):
        part = []
        cs = None
        for r in range(KEY_SEG - 1, -1, -1):
            w = jnp.exp2(jnp.minimum(zt[r * SUBLANES:(r + 1) * SUBLANES], SB_Z2_MAX))
            p = 1.0 / (1.0 + w)
            if rel is not None:
                m = (rel + seg * KEY_SEG + r) < lane
                w = jnp.where(m, w, 0.0)
                p = jnp.where(m, p, 1.0)
            cs = p if cs is None else p * cs
            part.append(w * cs)
        inc = cs
        for d in (1, 2, 4):
            sh = pltpu.roll(inc, SUBLANES - d, axis=0)
            inc = inc * jnp.where(seg + d < SUBLANES, sh, 1.0)
        off = jnp.where(seg < SUBLANES - 1, pltpu.roll(inc, SUBLANES - 1, axis=0), 1.0) * run
        a = [p_ * off for p_ in part[::-1]]
        return jnp.concatenate(a, axis=0), run * inc[0:1]

    def scores(j, z_ref, valid=None):
        j = jnp.maximum(j, 0)
        b = bias2 if valid is None else jnp.where(valid, bias2, -1e30)
        z_ref[...] = _dot(kp_ref[pl.ds(pl.multiple_of(j * bq, bq), bq), :], qt) + b

    def weights(z_ref, a_ref, masked, runs):
        runs = list(runs)
        for g in range(nsub - 1, -1, -1):
            for lt in range(nsub):
                cols = slice(lt * LANES, (lt + 1) * LANES)
                rows = slice(g * LANES, (g + 1) * LANES)
                if masked and g > lt:
                    a_ref[rows, cols] = jnp.zeros((LANES, LANES), BF16)
                    continue
                rel = 0 if (masked and g == lt) else None
                a, runs[lt] = chain(z_ref[rows, cols], rel, runs[lt])
                a_ref[rows, cols] = a.astype(BF16)
        return tuple(runs)

    def values(j, a_ref):
        acc_ref[...] += _dot(vt_ref[jnp.maximum(j, 0)], a_ref[...])

    scores(i, z1_ref)
    runs = weights(z1_ref, a1_ref, True, tuple(jnp.ones((1, LANES), F32) for _ in range(nsub)))
    scores(i - 1, z0_ref)

    def body(t, runs):
        ja = i - 1 - 2 * t
        jb = ja - 1
        values(ja + 1, a1_ref)
        scores(jb, z1_ref, jb >= 0)
        runs = weights(z0_ref, a0_ref, False, runs)
        values(ja, a0_ref)
        scores(jb - 1, z0_ref)
        return weights(z1_ref, a1_ref, False, runs)

    n_pairs = (i + 1) // 2
    lax.fori_loop(0, n_pairs, body, runs)
    values(i - 2 * n_pairs, a1_ref)
    o_ref[...] = (acc_ref[...].T * _silu(gate_ref[...])).astype(o_ref.dtype)


def _sb_prompt(bias, proj, q_sec, z_sec, kp, vt, bq):
    _, t, _ = proj.shape
    assert t % bq == 0 and bq % LANES == 0
    return pl.pallas_call(
        functools.partial(_sb_body, bq=bq),
        grid=(H_B, t // bq),
        in_specs=[
            pl.BlockSpec(memory_space=pltpu.SMEM),
            pl.BlockSpec((None, bq, D_B), lambda h, i: (q_sec, i, h)),
            pl.BlockSpec((None, bq, D_B), lambda h, i: (z_sec, i, h)),
            pl.BlockSpec((None, t, D_B), lambda h, i: (h, 0, 0)),
            pl.BlockSpec((None, t // bq, D_B, bq), lambda h, i: (h, 0, 0, 0)),
        ],
        out_specs=pl.BlockSpec((bq, D_B), lambda h, i: (i, h)),
        out_shape=jax.ShapeDtypeStruct((t, H_B * D_B), BF16),
        scratch_shapes=[pltpu.VMEM((D_B, bq), F32),
                        pltpu.VMEM((bq, bq), F32), pltpu.VMEM((bq, bq), F32),
                        pltpu.VMEM((bq, bq), BF16), pltpu.VMEM((bq, bq), BF16)],
        compiler_params=_cparams(("parallel", "arbitrary")),
        name="sb_prompt",
    )(bias, proj, proj, kp, vt)


def _sb_key_layouts(k, v, bq):
    t = k.shape[0]
    nb = t // LANES

    def interleave(a):
        a = a.astype(BF16).reshape(nb, SUBLANES, KEY_SEG, H_B, D_B)
        return jnp.transpose(a, (3, 0, 2, 1, 4))

    kp = interleave(k).reshape(H_B, t, D_B)
    vt = jnp.swapaxes(interleave(v).reshape(H_B, t // bq, bq, D_B), 2, 3)
    return kp, vt


def _conv_rows(xp_ref, w_ref, r0, rows, lo):
    acc = None
    for i in range(CONV_W):
        start = SUBLANES - (CONV_W - 1) + i + r0
        term = xp_ref[start:start + rows, lo:lo + LANES] * w_ref[i:i + 1, lo:lo + LANES]
        acc = term if acc is None else acc + term
    return acc


def _split_bf16(a):
    hi = a.astype(BF16)
    return hi, (a - hi.astype(F32)).astype(BF16)


def _dot_split(a, b):
    ah, al = a
    bh, bl = b
    return _dot(ah, bh) + (_dot(ah, bl) + _dot(al, bh))


def _tri_inverse_unit_lower(lmats, n):
    row = lax.broadcasted_iota(jnp.int32, (n, n), 0)
    col = lax.broadcasted_iota(jnp.int32, (n, n), 1)
    eye = jnp.where(row == col, 1.0, 0.0).astype(F32)
    ms = [-l for l in lmats]
    invs = [eye + m for m in ms]
    k = 2
    while k < n:
        splits = [_split_bf16(m) for m in ms]
        ms = [_dot_split(s, s) for s in splits]
        invs = [_dot_split(_split_bf16(iv), _split_bf16(eye + m)) for iv, m in zip(invs, ms)]
        k *= 2
    return invs


def _gdn_body(q_ref, k_ref, v_ref, z_ref, ab_ref, cw_ref, alog_ref, dtb_ref, gn_ref,
              o_ref, s_out_ref, tail_ref, xp_ref, s_ref, *, cpb):
    c = GDN_CHUNK
    rows = cpb * c
    step = pl.program_id(0)

    @pl.when(step == 0)
    def _():
        xp_ref[0:SUBLANES, :] = jnp.zeros((SUBLANES, CONV_DIM_A), F32)
        s_ref[...] = jnp.zeros_like(s_ref)

    xp_ref[SUBLANES:, 0:QK_A] = q_ref[...]
    xp_ref[SUBLANES:, QK_A:2 * QK_A] = k_ref[...]
    xp_ref[SUBLANES:, 2 * QK_A:] = v_ref[...]

    ab = ab_ref[...]
    g_all = -jnp.exp(alog_ref[...]) * _softplus(ab + dtb_ref[...])
    beta_all = _sigmoid(ab)
    ri = lax.broadcasted_iota(jnp.int32, (c, c), 0)
    ci = lax.broadcasted_iota(jnp.int32, (c, c), 1)
    tril = ri >= ci
    strict = ri > ci
    tri_f = jnp.where(tril, 1.0, 0.0).astype(F32)

    items = []
    for cc in range(cpb):
        r0 = cc * c
        gc = _dot_hi(tri_f, g_all[r0:r0 + c])
        gct = gc.T
        beta_c = beta_all[r0:r0 + c]
        for h in range(H_A):
            lo = h * DK_A
            qh = _silu(_conv_rows(xp_ref, cw_ref, r0, c, lo))
            kh = _silu(_conv_rows(xp_ref, cw_ref, r0, c, QK_A + lo))
            vh = _silu(_conv_rows(xp_ref, cw_ref, r0, c, 2 * QK_A + lo))
            qh = qh * lax.rsqrt(jnp.sum(qh * qh, axis=-1, keepdims=True) + 1e-6) * (DK_A ** -0.5)
            kh = kh * lax.rsqrt(jnp.sum(kh * kh, axis=-1, keepdims=True) + 1e-6)
            gcol = gc[:, h:h + 1]
            grow = gct[h:h + 1, :]
            beta = beta_c[:, H_A + h:H_A + h + 1]
            decay = jnp.where(tril, jnp.exp(jnp.where(tril, gcol - grow, 0.0)), 0.0)
            kb = kh.astype(BF16)
            lmat = jnp.where(strict, beta * _dot_nt(kb, kb) * decay, 0.0)
            egc = jnp.exp(gcol)
            glast = gcol[c - 1:c]
            items.append(dict(
                r0=r0, h=h, lmat=lmat,
                rhs=jnp.concatenate([vh * beta, kh * (beta * egc)], axis=-1),
                qk=jnp.where(tril, _dot_nt(qh.astype(BF16), kb) * decay, 0.0).astype(BF16),
                qg=(qh * egc).astype(BF16),
                kdec_t=(kh * jnp.exp(glast - gcol)).T.astype(BF16),
                sdecay=jnp.exp(glast)))
    invs = _tri_inverse_unit_lower([it["lmat"] for it in items], c)
    sols = [_dot_split(_split_bf16(iv), _split_bf16(it["rhs"])) for iv, it in zip(invs, items)]

    for it, sol in zip(items, sols):
        r0, h = it["r0"], it["h"]
        lo = h * DV_A
        s_old = s_ref[h]
        sb = s_old.astype(BF16)
        v_new = sol[:, :DV_A] - _dot(sol[:, DV_A:].astype(BF16), sb)
        vnb = v_new.astype(BF16)
        o = _dot(it["qg"], sb) + _dot(it["qk"], vnb)
        s_ref[h] = s_old * it["sdecay"] + _dot(it["kdec_t"], vnb)
        o = _rms(o, gn_ref[...]) * _silu(z_ref[r0:r0 + c, lo:lo + DV_A])
        o_ref[r0:r0 + c, lo:lo + DV_A] = o.astype(o_ref.dtype)

    xp_ref[0:SUBLANES, :] = xp_ref[rows:rows + SUBLANES, :]

    @pl.when(step == pl.num_programs(0) - 1)
    def _():
        s_out_ref[...] = s_ref[...]
        tail_ref[...] = xp_ref[0:SUBLANES, :]


def _gdn_prompt(proj, ab, conv_w, a_log_row, dt_bias_row, g_norm, cpb):
    _, t, _ = proj.shape
    rows = cpb * GDN_CHUNK
    assert t % rows == 0
    sec = lambda s: pl.BlockSpec((None, rows, SEC), lambda i: (s, i, 0))
    full = lambda shape: pl.BlockSpec(shape, lambda i: (0,) * len(shape))
    return pl.pallas_call(
        functools.partial(_gdn_body, cpb=cpb),
        grid=(t // rows,),
        in_specs=[sec(0), sec(1), sec(2), sec(3),
                  pl.BlockSpec((rows, LANES), lambda i: (i, 0)),
                  full((CONV_W, CONV_DIM_A)), full((1, LANES)), full((1, LANES)), full((1, DV_A))],
        out_specs=[pl.BlockSpec((rows, V_A), lambda i: (i, 0)),
                   full((H_A, DK_A, DV_A)), full((SUBLANES, CONV_DIM_A))],
        out_shape=[jax.ShapeDtypeStruct((t, V_A), BF16),
                   jax.ShapeDtypeStruct((H_A, DK_A, DV_A), F32),
                   jax.ShapeDtypeStruct((SUBLANES, CONV_DIM_A), F32)],
        scratch_shapes=[pltpu.VMEM((rows + SUBLANES, CONV_DIM_A), F32),
                        pltpu.VMEM((H_A, DK_A, DV_A), F32)],
        compiler_params=_cparams(("arbitrary",)),
        name="gdn_prompt",
    )(proj, proj, proj, proj, ab, conv_w, a_log_row, dt_bias_row, g_norm)


def _ssd_body(z0_ref, z1_ref, x0_ref, x1_ref, bc_ref, dt_ref, cw_ref, cb_ref, alog_ref, dtb_ref,
              dskip_ref, gn_ref, exp_ref, o_ref, s_out_ref, tail_ref, xp_ref, s_ref, *, cpb):
    c = SSD_CHUNK
    rows = cpb * c
    step = pl.program_id(0)
    gw = D_INNER_C // N_GROUPS_C
    n_bc = N_GROUPS_C * D_STATE_C

    @pl.when(step == 0)
    def _():
        xp_ref[0:SUBLANES, :] = jnp.zeros((SUBLANES, CONV_DIM_C), F32)
        s_ref[...] = jnp.zeros_like(s_ref)

    xp_ref[SUBLANES:, 0:SEC] = x0_ref[...]
    xp_ref[SUBLANES:, SEC:2 * SEC] = x1_ref[...]
    xp_ref[SUBLANES:, 2 * SEC:] = bc_ref[...]

    dt_all = _softplus(dt_ref[...] + dtb_ref[...])
    da_all = dt_all * (-jnp.exp(alog_ref[...]))
    ri = lax.broadcasted_iota(jnp.int32, (c, c), 0)
    ci = lax.broadcasted_iota(jnp.int32, (c, c), 1)
    tril = ri >= ci
    tri_f = jnp.where(tril, 1.0, 0.0).astype(F32)
    lane = lax.broadcasted_iota(jnp.int32, (1, LANES), 1)
    left = lane < HEAD_DIM_C
    rowi = lax.broadcasted_iota(jnp.int32, (LANES, 1), 0)
    top = rowi < HEAD_DIM_C
    expand = exp_ref[...]

    for cc in range(cpb):
        r0 = cc * c
        dt = dt_all[r0:r0 + c]
        acum = _dot_hi(tri_f, da_all[r0:r0 + c])
        acum_t = acum.T
        dt_x = _select_dot(dt, expand)
        ac_x = _select_dot(acum, expand)
        for g in range(N_GROUPS_C):
            bm = _silu(_conv_rows(xp_ref, cw_ref, r0, c, D_INNER_C + g * D_STATE_C)
                       + cb_ref[:, D_INNER_C + g * D_STATE_C:D_INNER_C + (g + 1) * D_STATE_C])
            cm = _silu(_conv_rows(xp_ref, cw_ref, r0, c, D_INNER_C + n_bc + g * D_STATE_C)
                       + cb_ref[:, D_INNER_C + n_bc + g * D_STATE_C:D_INNER_C + n_bc + (g + 1) * D_STATE_C])
            bmb = bm.astype(BF16)
            cmb = cm.astype(BF16)
            cb = _dot_nt(cmb, bmb)
            ys = []
            for pr in range(gw // LANES):
                lo = g * gw + pr * LANES
                h0 = lo // HEAD_DIM_C
                xs = _silu(_conv_rows(xp_ref, cw_ref, r0, c, lo) + cb_ref[:, lo:lo + LANES])
                xdt = xs * dt_x[:, lo:lo + LANES]
                acx = ac_x[:, lo:lo + LANES]
                y = None
                for hh in range(2):
                    hd = h0 + hh
                    seg = acum[:, hd:hd + 1] - acum_t[hd:hd + 1, :]
                    lm = jnp.where(tril, jnp.exp(jnp.where(tril, seg, 0.0)), 0.0)
                    xm = jnp.where(left if hh == 0 else jnp.logical_not(left), xdt, 0.0)
                    term = _dot((cb * lm).astype(BF16), xm.astype(BF16))
                    y = term if y is None else y + term
                s_old = s_ref[lo:lo + LANES, :]
                y = y + _dot_nt(cmb, s_old.astype(BF16)) * jnp.exp(acx)
                dte = jnp.exp(acx[c - 1:c] - acx)
                st = _dot((xdt * dte).T.astype(BF16), bmb)
                al0 = acum[c - 1:c, h0:h0 + 1]
                al1 = acum[c - 1:c, h0 + 1:h0 + 2]
                s_ref[lo:lo + LANES, :] = s_old * jnp.exp(jnp.where(top, al0, al1)) + st
                y = y + xs * dskip_ref[:, lo:lo + LANES]
                zz = (z0_ref if lo < SEC else z1_ref)[r0:r0 + c, (lo % SEC):(lo % SEC) + LANES]
                ys.append(y * _silu(zz))
            yg = jnp.concatenate(ys, axis=-1)
            yg = _rms(yg, gn_ref[:, g * gw:(g + 1) * gw])
            o_ref[r0:r0 + c, g * gw:(g + 1) * gw] = yg.astype(o_ref.dtype)

    xp_ref[0:SUBLANES, :] = xp_ref[rows:rows + SUBLANES, :]

    @pl.when(step == pl.num_programs(0) - 1)
    def _():
        s_out_ref[...] = s_ref[...]
        tail_ref[...] = xp_ref[0:SUBLANES, :]


def _ssd_prompt(proj, dt_raw, conv_w, conv_b, a_log_row, dt_bias_row, dskip_x, g_norm, expand, cpb):
    _, t, _ = proj.shape
    rows = cpb * SSD_CHUNK
    assert t % rows == 0
    sec = lambda s: pl.BlockSpec((None, rows, SEC), lambda i: (s, i, 0))
    full = lambda shape: pl.BlockSpec(shape, lambda i: (0,) * len(shape))
    return pl.pallas_call(
        functools.partial(_ssd_body, cpb=cpb),
        grid=(t // rows,),
        in_specs=[sec(0), sec(1), sec(2), sec(3), sec(4),
                  pl.BlockSpec((rows, LANES), lambda i: (i, 0)),
                  full((CONV_W, CONV_DIM_C)), full((1, CONV_DIM_C)), full((1, LANES)), full((1, LANES)),
                  full((1, D_INNER_C)), full((1, D_INNER_C)), full((LANES, D_INNER_C))],
        out_specs=[pl.BlockSpec((rows, D_INNER_C), lambda i: (i, 0)),
                   full((D_INNER_C, D_STATE_C)), full((SUBLANES, CONV_DIM_C))],
        out_shape=[jax.ShapeDtypeStruct((t, D_INNER_C), BF16),
                   jax.ShapeDtypeStruct((D_INNER_C, D_STATE_C), F32),
                   jax.ShapeDtypeStruct((SUBLANES, CONV_DIM_C), F32)],
        scratch_shapes=[pltpu.VMEM((rows + SUBLANES, CONV_DIM_C), F32),
                        pltpu.VMEM((D_INNER_C, D_STATE_C), F32)],
        compiler_params=_cparams(("arbitrary",)),
        name="ssd_prompt",
    )(proj, proj, proj, proj, proj, dt_raw, conv_w, conv_b, a_log_row, dt_bias_row, dskip_x, g_norm, expand)


def _head_expand():
    r = jnp.arange(LANES)[:, None]
    cidx = jnp.arange(D_INNER_C)[None, :]
    return (cidx // HEAD_DIM_C == r).astype(BF16)


def _row_to_col(row, eye):
    return jnp.sum(jnp.where(eye, row, 0.0), axis=1, keepdims=True)


def _col_to_row(col, eye):
    return jnp.sum(jnp.where(eye, col, 0.0), axis=0, keepdims=True)


def _eye(n):
    return lax.broadcasted_iota(jnp.int32, (n, n), 0) == lax.broadcasted_iota(jnp.int32, (n, n), 1)


def _conv_step(x_row, buf_ref, w_ref, lo, hi):
    acc = x_row * w_ref[CONV_W - 1:CONV_W, lo:hi]
    for i in range(CONV_W - 1):
        acc = acc + buf_ref[i:i + 1, lo:hi] * w_ref[i:i + 1, lo:hi]
    return acc


def _gdn_step_body(q_ref, k_ref, v_ref, z_ref, ab_ref, buf_ref, s_ref, cw_ref, alog_ref, dtb_ref, gn_ref,
                   o_ref, s_out_ref, buf_out_ref):
    eye = _eye(LANES)
    ab = ab_ref[...]
    eg_all = jnp.exp(-jnp.exp(alog_ref[...]) * _softplus(ab + dtb_ref[...]))
    beta_all = _sigmoid(ab)
    buf_out_ref[0:CONV_W - 2, :] = buf_ref[1:CONV_W - 1, :]
    pre = (q_ref, k_ref, v_ref)
    for n in range(3):
        buf_out_ref[CONV_W - 2:CONV_W - 1, n * SEC:(n + 1) * SEC] = pre[n][...]
    for h in range(H_A):
        lo = h * DK_A
        q = _silu(_conv_step(q_ref[:, lo:lo + DK_A], buf_ref, cw_ref, lo, lo + DK_A))
        k = _silu(_conv_step(k_ref[:, lo:lo + DK_A], buf_ref, cw_ref, QK_A + lo, QK_A + lo + DK_A))
        v = _silu(_conv_step(v_ref[:, lo:lo + DV_A], buf_ref, cw_ref, 2 * QK_A + lo, 2 * QK_A + lo + DV_A))
        q = q * lax.rsqrt(jnp.sum(q * q, axis=-1, keepdims=True) + 1e-6) * (DK_A ** -0.5)
        k = k * lax.rsqrt(jnp.sum(k * k, axis=-1, keepdims=True) + 1e-6)
        eg = eg_all[:, h:h + 1]
        beta = beta_all[:, H_A + h:H_A + h + 1]
        s0 = s_ref[h]
        kc = _row_to_col(k, eye)
        qc = _row_to_col(q, eye)
        sk = jnp.sum(s0 * kc, axis=0, keepdims=True)
        sq = jnp.sum(s0 * qc, axis=0, keepdims=True)
        v_new = beta * (v - eg * sk)
        o = eg * sq + jnp.sum(q * k, axis=-1, keepdims=True) * v_new
        s_out_ref[h] = s0 * eg + kc * v_new
        o_ref[:, lo:lo + DV_A] = _rms(o, gn_ref[...]) * _silu(z_ref[:, lo:lo + DV_A])


def _gdn_step(proj, ab, conv_buf, state, conv_w, a_log_row, dt_bias_row, g_norm):
    b = ab.shape[0]
    sec = lambda s: pl.BlockSpec((None, None, 1, SEC), lambda i: (s, i, 0, 0))
    full = lambda shape: pl.BlockSpec(shape, lambda i: (0,) * len(shape))
    return pl.pallas_call(
        _gdn_step_body,
        grid=(b,),
        in_specs=[sec(0), sec(1), sec(2), sec(3),
                  pl.BlockSpec((None, 1, LANES), lambda i: (i, 0, 0)),
                  pl.BlockSpec((None, CONV_W - 1, CONV_DIM_A), lambda i: (i, 0, 0)),
                  pl.BlockSpec((None, H_A, DK_A, DV_A), lambda i: (i, 0, 0, 0)),
                  full((CONV_W, CONV_DIM_A)), full((1, LANES)), full((1, LANES)), full((1, DV_A))],
        out_specs=[pl.BlockSpec((None, 1, V_A), lambda i: (i, 0, 0)),
                   pl.BlockSpec((None, H_A, DK_A, DV_A), lambda i: (i, 0, 0, 0)),
                   pl.BlockSpec((None, CONV_W - 1, CONV_DIM_A), lambda i: (i, 0, 0))],
        out_shape=[jax.ShapeDtypeStruct((b, 1, V_A), F32),
                   jax.ShapeDtypeStruct((b, H_A, DK_A, DV_A), F32),
                   jax.ShapeDtypeStruct((b, CONV_W - 1, CONV_DIM_A), F32)],
        compiler_params=_cparams(("parallel",)),
        name="gdn_step",
    )(proj, proj, proj, proj, ab, conv_buf, state, conv_w, a_log_row, dt_bias_row, g_norm)


def _ssd_step_body(z0_ref, z1_ref, x0_ref, x1_ref, bc_ref, dt_ref, buf_ref, s_ref, cw_ref, cb_ref, alog_ref,
                   dtb_ref, dskip_ref, gn_ref, o_ref, s_out_ref, buf_out_ref):
    eye = _eye(LANES)
    gw = D_INNER_C // N_GROUPS_C
    n_bc = N_GROUPS_C * D_STATE_C
    top = lax.broadcasted_iota(jnp.int32, (LANES, 1), 0) < HEAD_DIM_C
    dt_all = _softplus(dt_ref[...] + dtb_ref[...])
    dec_all = jnp.exp(dt_all * (-jnp.exp(alog_ref[...])))
    buf_out_ref[0:CONV_W - 2, :] = buf_ref[1:CONV_W - 1, :]
    pre = (x0_ref, x1_ref, bc_ref)
    for n in range(3):
        buf_out_ref[CONV_W - 2:CONV_W - 1, n * SEC:(n + 1) * SEC] = pre[n][...]

    def conv(ref, lo_in, lo):
        return _silu(_conv_step(ref[:, lo_in:lo_in + LANES], buf_ref, cw_ref, lo, lo + LANES)
                     + cb_ref[:, lo:lo + LANES])

    for g in range(N_GROUPS_C):
        bm = conv(bc_ref, g * D_STATE_C, D_INNER_C + g * D_STATE_C)
        cm = conv(bc_ref, n_bc + g * D_STATE_C, D_INNER_C + n_bc + g * D_STATE_C)
        ys = []
        for pr in range(gw // LANES):
            lo = g * gw + pr * LANES
            h0 = lo // HEAD_DIM_C
            xs = conv(x0_ref if lo < SEC else x1_ref, lo % SEC, lo)
            xc = _row_to_col(xs, eye)
            dt_c = jnp.where(top, dt_all[:, h0:h0 + 1], dt_all[:, h0 + 1:h0 + 2])
            dec_c = jnp.where(top, dec_all[:, h0:h0 + 1], dec_all[:, h0 + 1:h0 + 2])
            h_new = s_ref[lo:lo + LANES, :] * dec_c + (xc * dt_c) * bm
            s_out_ref[lo:lo + LANES, :] = h_new
            y = _col_to_row(jnp.sum(h_new * cm, axis=1, keepdims=True), eye)
            y = y + xs * dskip_ref[:, lo:lo + LANES]
            zz = (z0_ref if lo < SEC else z1_ref)[:, (lo % SEC):(lo % SEC) + LANES]
            ys.append(y * _silu(zz))
        yg = jnp.concatenate(ys, axis=-1)
        o_ref[:, g * gw:(g + 1) * gw] = _rms(yg, gn_ref[:, g * gw:(g + 1) * gw])


def _ssd_step(proj, dt_raw, conv_buf, state, conv_w, conv_b, a_log_row, dt_bias_row, dskip_x, g_norm):
    b = dt_raw.shape[0]
    sec = lambda s: pl.BlockSpec((None, None, 1, SEC), lambda i: (s, i, 0, 0))
    full = lambda shape: pl.BlockSpec(shape, lambda i: (0,) * len(shape))
    return pl.pallas_call(
        _ssd_step_body,
        grid=(b,),
        in_specs=[sec(0), sec(1), sec(2), sec(3), sec(4),
                  pl.BlockSpec((None, 1, LANES), lambda i: (i, 0, 0)),
                  pl.BlockSpec((None, CONV_W - 1, CONV_DIM_C), lambda i: (i, 0, 0)),
                  pl.BlockSpec((None, D_INNER_C, D_STATE_C), lambda i: (i, 0, 0)),
                  full((CONV_W, CONV_DIM_C)), full((1, CONV_DIM_C)), full((1, LANES)), full((1, LANES)),
                  full((1, D_INNER_C)), full((1, D_INNER_C))],
        out_specs=[pl.BlockSpec((None, 1, D_INNER_C), lambda i: (i, 0, 0)),
                   pl.BlockSpec((None, D_INNER_C, D_STATE_C), lambda i: (i, 0, 0)),
                   pl.BlockSpec((None, CONV_W - 1, CONV_DIM_C), lambda i: (i, 0, 0))],
        out_shape=[jax.ShapeDtypeStruct((b, 1, D_INNER_C), F32),
                   jax.ShapeDtypeStruct((b, D_INNER_C, D_STATE_C), F32),
                   jax.ShapeDtypeStruct((b, CONV_W - 1, CONV_DIM_C), F32)],
        compiler_params=_cparams(("parallel",)),
        name="ssd_step",
    )(proj, proj, proj, proj, proj, dt_raw, conv_buf, state, conv_w, conv_b, a_log_row, dt_bias_row,
      dskip_x, g_norm)


SB_PAGES_PER_STEP = 16


def _sb_paged_body(pt_ref, q_ref, z_ref, bias_ref, *refs, ppb):
    del pt_ref
    k_refs = refs[:ppb]
    v_refs = refs[ppb:2 * ppb]
    o_ref, acc_ref, run_ref = refs[2 * ppb:]
    p = pl.program_id(1)
    grp = PAGE_SIZE // KEY_SEG
    rows = KEY_SEG * H_B

    @pl.when(p == 0)
    def _():
        acc_ref[...] = jnp.zeros_like(acc_ref)
        run_ref[...] = jnp.zeros_like(run_ref)

    sub = lax.broadcasted_iota(jnp.int32, (SUBLANES, LANES), 0)
    lane = lax.broadcasted_iota(jnp.int32, (SUBLANES, LANES), 1)
    own = (lane % H_B) == sub
    ri = lax.broadcasted_iota(jnp.int32, (rows, 2 * rows), 0)
    ci = lax.broadcasted_iota(jnp.int32, (rows, 2 * rows), 1)
    same_head = (ri % H_B) == (ci % H_B)
    tmat = jnp.where(same_head & ((ci >= rows) | (ri // H_B >= ci // H_B)), 1.0, 0.0).astype(BF16)
    qb = q_ref[...].astype(BF16)
    bias = bias_ref[...]
    pages = range(ppb)
    k2 = [k_refs[r][...].reshape(PAGE_SIZE * H_B, D_B).astype(BF16) for r in pages]
    zs = []
    for r in pages:
        zrows = []
        for g in range(grp):
            zt = _dot_nt(qb, k2[r][g * rows:(g + 1) * rows])
            zrows.append(jnp.sum(jnp.where(own, zt, 0.0), axis=0, keepdims=True))
        zs.append(jnp.concatenate(zrows, axis=0) * (D_B ** -0.5) + bias)
    sps = [_split_bf16(_softplus(z)) for z in zs]
    css = [_dot(hi, tmat) + _dot(lo, tmat) for hi, lo in sps]
    offs = []
    tots = []
    for cs in css:
        inc = cs[:, rows:]
        for d in (1, 2, 4):
            sh = pltpu.roll(inc, SUBLANES - d, axis=0)
            inc = inc + jnp.where(sub + d < SUBLANES, sh, 0.0)
        offs.append(jnp.where(sub < SUBLANES - 1, pltpu.roll(inc, SUBLANES - 1, axis=0), 0.0))
        tots.append(inc[0:1])
    run_after = run_ref[...]
    acc = acc_ref[...]
    for r in pages:
        a = jnp.exp(zs[r] - (css[r][:, :rows] + (offs[r] + run_after)))
        lhs = jnp.concatenate([jnp.where(own, a[g:g + 1], 0.0) for g in range(grp)], axis=1)
        v2 = v_refs[r][...].reshape(PAGE_SIZE * H_B, D_B).astype(BF16)
        acc = acc + _dot(lhs.astype(BF16), v2)
        run_after = run_after + tots[r]
    acc_ref[...] = acc
    run_ref[...] = run_after

    @pl.when(p == pl.num_programs(1) - 1)
    def _():
        o_ref[...] = acc * _silu(z_ref[...])


def _sb_paged(page_table, q, zgate, bias_row, cache_k, cache_v, layer):
    b, npg = page_table.shape
    ppb = SB_PAGES_PER_STEP
    assert npg % ppb == 0

    def page_spec(r):
        return pl.BlockSpec((None, None, PAGE_SIZE, H_B, D_B),
                            lambda i, p, pt: (layer, pt[i, npg - 1 - (p * ppb + r)], 0, 0, 0))

    row = pl.BlockSpec((None, H_B, D_B), lambda i, p, pt: (i, 0, 0))
    grid_spec = pltpu.PrefetchScalarGridSpec(
        num_scalar_prefetch=1,
        grid=(b, npg // ppb),
        in_specs=[row, row, pl.BlockSpec((1, LANES), lambda i, p, pt: (0, 0))]
        + [page_spec(r) for r in range(ppb)] * 2,
        out_specs=row,
        scratch_shapes=[pltpu.VMEM((H_B, D_B), F32), pltpu.VMEM((1, LANES), F32)],
    )
    return pl.pallas_call(
        functools.partial(_sb_paged_body, ppb=ppb),
        grid_spec=grid_spec,
        out_shape=jax.ShapeDtypeStruct((b, H_B, D_B), F32),
        compiler_params=_cparams(("parallel", "arbitrary")),
        name="sb_paged",
    )(page_table, q, zgate, bias_row, *([cache_k] * ppb), *([cache_v] * ppb))


def _pad_lanes(v):
    return jnp.pad(v.astype(F32), (0, LANES - v.shape[0])).reshape(1, LANES)


def _even_weights(w_in):
    o_ab = CONV_DIM_A
    o_z = o_ab + 2 * H_A
    w_main = jnp.concatenate([w_in[:, :o_ab], w_in[:, o_z:]], axis=1).astype(BF16)
    w_sec = jnp.transpose(w_main.reshape(D_MODEL, 8, SEC), (1, 0, 2))
    w_small = jnp.pad(w_in[:, o_ab:o_z], ((0, 0), (0, LANES - 2 * H_A))).astype(BF16)
    return w_sec, w_small


def _odd_weights(w_in):
    n_main = D_INNER_C + CONV_DIM_C
    w_sec = jnp.transpose(w_in[:, :n_main].astype(BF16).reshape(D_MODEL, 5, SEC), (1, 0, 2))
    w_small = jnp.pad(w_in[:, n_main:], ((0, 0), (0, LANES - H_C))).astype(BF16)
    return w_sec, w_small


def kernel(x_prompt, x_sample, cache_sb_k, cache_sb_v, state_gdn, state_gdn_conv, state_ssd, state_ssd_conv, page_table, even_norm_pre, even_norm_post, even_w_in, gdn_conv_w, gdn_a_log, gdn_dt_bias, gdn_norm_w, sb_bias, even_w_out, odd_norm_pre, odd_norm_post, odd_w_in, ssd_conv_w, ssd_conv_b, ssd_a_log, ssd_dt_bias, ssd_d, ssd_norm_w, odd_w_out):
    t = x_prompt.shape[1]
    bs = x_sample.shape[0]
    xp = x_prompt[0]
    xs = x_sample[:, 0]
    tail = SUBLANES - (CONV_W - 1)
    tm_p = 1024
    tm_s = bs

    w_sec, w_small = _even_weights(even_w_in[0])
    w_out_e = even_w_out[0].astype(BF16)
    a_log_e = _pad_lanes(gdn_a_log[0])
    dt_bias_e = _pad_lanes(gdn_dt_bias[0])
    proj, ab, k_b, v_b = _proj(xp, even_norm_pre, w_sec, w_small, tm_p, own=(5, 6))
    sb_k_prompt = k_b.reshape(1, 1, t, H_B, D_B)
    sb_v_prompt = v_b.reshape(1, 1, t, H_B, D_B)
    kp, vt = _sb_key_layouts(k_b, v_b, SB_BQ)
    o_b = _sb_prompt(sb_bias[0], proj, 4, 5, kp, vt, SB_BQ)
    o_a, gdn_s, gdn_tail = _gdn_prompt(proj, ab, gdn_conv_w[0], a_log_e, dt_bias_e, gdn_norm_w, 4)
    y1 = _outproj(o_a, o_b, w_out_e, even_norm_post, xp, tm_p)
    proj_s, ab_s, k_b_s, v_b_s = _proj(xs, even_norm_pre, w_sec, w_small, tm_s, own=(5, 6))
    sb_k_sample = k_b_s.reshape(1, bs, 1, H_B, D_B)
    sb_v_sample = v_b_s.reshape(1, bs, 1, H_B, D_B)
    o_b_s = _sb_paged(page_table, proj_s[4].reshape(bs, H_B, D_B), proj_s[5].reshape(bs, H_B, D_B),
                      jnp.tile(sb_bias[0].astype(F32), LANES // H_B).reshape(1, LANES),
                      cache_sb_k, cache_sb_v, 0)
    o_a_s, gdn_s_s, gdn_conv_s = _gdn_step(proj_s.reshape(6, bs, 1, SEC), ab_s.reshape(bs, 1, LANES),
                                           state_gdn_conv[0], state_gdn[0], gdn_conv_w[0], a_log_e,
                                           dt_bias_e, gdn_norm_w)
    y1_s = _outproj(o_a_s.reshape(bs, V_A), o_b_s.reshape(bs, W_B), w_out_e, even_norm_post, xs, tm_s)

    w_sec_o, w_small_o = _odd_weights(odd_w_in[0])
    w_out_o = odd_w_out[0].astype(BF16)
    a_log_o = _pad_lanes(ssd_a_log[0])
    dt_bias_o = _pad_lanes(ssd_dt_bias[0])
    dskip_x = jnp.repeat(ssd_d[0].astype(F32), HEAD_DIM_C).reshape(1, D_INNER_C)
    proj_o, dt_raw = _proj(y1, odd_norm_pre, w_sec_o, w_small_o, tm_p)
    y_ssd, ssd_s, ssd_tail = _ssd_prompt(proj_o, dt_raw, ssd_conv_w[0], ssd_conv_b, a_log_o, dt_bias_o,
                                         dskip_x, ssd_norm_w, _head_expand(), 2)
    y2 = _outproj(y_ssd, y_ssd, w_out_o, odd_norm_post, y1, tm_p, 0, 1)
    proj_os, dt_raw_s = _proj(y1_s, odd_norm_pre, w_sec_o, w_small_o, tm_s)
    y_ssd_s, ssd_s_s, ssd_conv_s = _ssd_step(proj_os.reshape(5, bs, 1, SEC), dt_raw_s.reshape(bs, 1, LANES),
                                             state_ssd_conv[0], state_ssd[0].reshape(bs, D_INNER_C, D_STATE_C),
                                             ssd_conv_w[0], ssd_conv_b, a_log_o, dt_bias_o, dskip_x, ssd_norm_w)
    y_ssd_s = y_ssd_s.reshape(bs, D_INNER_C)
    y2_s = _outproj(y_ssd_s, y_ssd_s, w_out_o, odd_norm_post, y1_s, tm_s, 0, 1)

    return (y2[None], y2_s[:, None], sb_k_prompt, sb_v_prompt, sb_k_sample, sb_v_sample,
            gdn_s[None, None], gdn_s_s[None], gdn_tail[tail:][None, None], gdn_conv_s[None],
            ssd_s.reshape(1, 1, H_C, HEAD_DIM_C, D_STATE_C),
            ssd_s_s.reshape(1, bs, H_C, HEAD_DIM_C, D_STATE_C),
            ssd_tail[tail:][None, None], ssd_conv_s[None])
```

```python
import functools

import jax
import jax.numpy as jnp
from jax import lax
from jax.experimental import pallas as pl
from jax.experimental.pallas import tpu as pltpu

F32 = jnp.float32
BF16 = jnp.bfloat16
HIGHEST = lax.Precision.HIGHEST

LANES = 128
SUBLANES = 8
VMEM_LIMIT_BYTES = 48 * 1024 * 1024

D_MODEL = 1024
H_A = 8
DK_A = 128
DV_A = 128
H_B = 8
D_B = 128
CONV_W = 4
GDN_CHUNK = 64
QK_A = H_A * DK_A
V_A = H_A * DV_A
W_B = H_B * D_B
CONV_DIM_A = 2 * QK_A + V_A
D_INNER_C = 2 * D_MODEL
HEAD_DIM_C = 64
H_C = D_INNER_C // HEAD_DIM_C
N_GROUPS_C = 4
D_STATE_C = 128
SSD_CHUNK = 64
CONV_DIM_C = D_INNER_C + 2 * N_GROUPS_C * D_STATE_C
PAGE_SIZE = 128
RMS_EPS = 1e-6
SEC = 1024


def _cparams(sem):
    return pltpu.CompilerParams(dimension_semantics=sem, vmem_limit_bytes=VMEM_LIMIT_BYTES)


def _dot(a, b):
    return jnp.dot(a, b, preferred_element_type=F32)


def _dot_nt(a, b):
    return lax.dot_general(a, b, (((1,), (1,)), ((), ())), preferred_element_type=F32)


def _dot_hi(a, b):
    return jnp.dot(a, b, preferred_element_type=F32, precision=HIGHEST)


def _split3_bf16(x):
    hi = x.astype(BF16)
    r1 = x - hi.astype(F32)
    mid = r1.astype(BF16)
    return hi, mid, (r1 - mid.astype(F32)).astype(BF16)


def _select_dot(x, sel):
    hi, mid, lo = _split3_bf16(x)
    return _dot(hi, sel) + (_dot(mid, sel) + _dot(lo, sel))


def _sigmoid(x):
    return 1.0 / (1.0 + jnp.exp(-x))


def _silu(x):
    return x * _sigmoid(x)


def _softplus(x):
    return jnp.maximum(x, 0.0) + jnp.log1p(jnp.exp(-jnp.abs(x)))


def _rms(x, w):
    return x * lax.rsqrt(jnp.mean(x * x, axis=-1, keepdims=True) + RMS_EPS) * w


def _proj_body(x_ref, nw_ref, w_ref, ws_ref, o_ref, os_ref, *rest, own):
    own_refs, h_ref = rest[:-1], rest[-1]
    j = pl.program_id(1)

    @pl.when(j == 0)
    def _():
        hb = _rms(x_ref[...], nw_ref[...]).astype(BF16)
        h_ref[...] = hb
        os_ref[...] = _dot(hb, ws_ref[...])

    res = _dot(h_ref[...], w_ref[...])
    stacked = None
    for sec, ref in zip(own, own_refs):
        @pl.when(j == sec)
        def _(ref=ref):
            ref[...] = res

        stacked = (j != sec) if stacked is None else stacked & (j != sec)
    if stacked is None:
        o_ref[...] = res
    else:
        @pl.when(stacked)
        def _():
            o_ref[...] = res


def _proj(x, nw, w_sec, w_small, tm, own=()):
    m, d = x.shape
    s = w_sec.shape[0]
    assert m % tm == 0

    def stacked_index(j):
        return j - sum((j >= e).astype(jnp.int32) for e in own) if own else j

    own_spec = pl.BlockSpec((tm, SEC), lambda i, j: (i, 0))
    return pl.pallas_call(
        functools.partial(_proj_body, own=own),
        grid=(m // tm, s),
        in_specs=[
            pl.BlockSpec((tm, d), lambda i, j: (i, 0)),
            pl.BlockSpec((1, d), lambda i, j: (0, 0)),
            pl.BlockSpec((None, d, SEC), lambda i, j: (j, 0, 0)),
            pl.BlockSpec((d, LANES), lambda i, j: (0, 0)),
        ],
        out_specs=[
            pl.BlockSpec((None, tm, SEC), lambda i, j: (stacked_index(j), i, 0)),
            pl.BlockSpec((tm, LANES), lambda i, j: (i, 0)),
        ] + [own_spec] * len(own),
        out_shape=[
            jax.ShapeDtypeStruct((s - len(own), m, SEC), F32),
            jax.ShapeDtypeStruct((m, LANES), F32),
        ] + [jax.ShapeDtypeStruct((m, SEC), F32)] * len(own),
        scratch_shapes=[pltpu.VMEM((tm, d), BF16)],
        compiler_params=_cparams(("parallel", "arbitrary")),
        name="proj",
    )(x, nw, w_sec, w_small)


def _outproj_body(ya_ref, yb_ref, wa_ref, wb_ref, nw_ref, x_ref, o_ref):
    acc = _dot(ya_ref[...].astype(BF16), wa_ref[...]) + _dot(yb_ref[...].astype(BF16), wb_ref[...])
    o_ref[...] = x_ref[...] + _rms(acc, nw_ref[...])


def _outproj(ya, yb, w, nw, x, tm, ya_col=0, yb_col=0):
    m, d = x.shape
    assert m % tm == 0
    return pl.pallas_call(
        _outproj_body,
        grid=(m // tm,),
        in_specs=[
            pl.BlockSpec((tm, SEC), lambda i: (i, ya_col)),
            pl.BlockSpec((tm, SEC), lambda i: (i, yb_col)),
            pl.BlockSpec((SEC, d), lambda i: (0, 0)),
            pl.BlockSpec((SEC, d), lambda i: (1, 0)),
            pl.BlockSpec((1, d), lambda i: (0, 0)),
            pl.BlockSpec((tm, d), lambda i: (i, 0)),
        ],
        out_specs=pl.BlockSpec((tm, d), lambda i: (i, 0)),
        out_shape=jax.ShapeDtypeStruct((m, d), F32),
        compiler_params=_cparams(("parallel",)),
        name="outproj",
    )(ya, yb, w, w, nw, x)


KEY_SEG = LANES // SUBLANES
LOG2E = 1.4426950408889634
SB_BQ = 512
SB_Z2_MAX = 64.0


def _sb_body(bias_ref, q_ref, gate_ref, kp_ref, vt_ref, o_ref, acc_ref, z0_ref, z1_ref, z2_ref,
             a0_ref, a1_ref, a2_ref, *, bq):
    h = pl.program_id(0)
    i = pl.program_id(1)
    nsub = bq // LANES
    last_block = pl.num_programs(1) - 1
    qt = (q_ref[...] * (D_B ** -0.5 * LOG2E)).T.astype(BF16)
    bias2 = bias_ref[h] * LOG2E
    seg = lax.broadcasted_iota(jnp.int32, (SUBLANES, LANES), 0)
    lane = lax.broadcasted_iota(jnp.int32, (SUBLANES, LANES), 1)
    acc_ref[...] = jnp.zeros_like(acc_ref)
    a2_ref[...] = jnp.zeros_like(a2_ref)

    def chain(zt, rel, run):
        part = []
        cs = None
        for r in range(KEY_SEG - 1, -1, -1):
            w = jnp.exp2(jnp.minimum(zt[r * SUBLANES:(r + 1) * SUBLANES], SB_Z2_MAX))
            p = 1.0 / (1.0 + w)
            if rel is not None:
                m = (rel + seg * KEY_SEG + r) < lane
                w = jnp.where(m, w, 0.0)
                p = jnp.where(m, p, 1.0)
            cs = p if cs is None else p * cs
            part.append(w * cs)
        inc = cs
        for d in (1, 2, 4):
            sh = pltpu.roll(inc, SUBLANES - d, axis=0)
            inc = inc * jnp.where(seg + d < SUBLANES, sh, 1.0)
        off = jnp.where(seg < SUBLANES - 1, pltpu.roll(inc, SUBLANES - 1, axis=0), 1.0) * run
        a = [p_ * off for p_ in part[::-1]]
        return jnp.concatenate(a, axis=0), run * inc[0:1]

    def scores(j, z_ref, valid=None):
        j = jnp.maximum(j, 0)
        b = bias2 if valid is None else jnp.where(valid, bias2, -1e30)
        z_ref[...] = _dot(kp_ref[pl.ds(pl.multiple_of(j * bq, bq), bq), :], qt) + b

    def weights(z_ref, a_ref, masked, runs):
        runs = list(runs)
        for g in range(nsub - 1, -1, -1):
            for lt in range(nsub):
                cols = slice(lt * LANES, (lt + 1) * LANES)
                rows = slice(g * LANES, (g + 1) * LANES)
                if masked and g > lt:
                    a_ref[rows, cols] = jnp.zeros((LANES, LANES), BF16)
                    continue
                rel = 0 if (masked and g == lt) else None
                a, runs[lt] = chain(z_ref[rows, cols], rel, runs[lt])
                a_ref[rows, cols] = a.astype(BF16)
        return tuple(runs)

    def values(j, a_ref):
        acc_ref[...] += _dot(vt_ref[jnp.clip(j, 0, last_block)], a_ref[...])

    zs = (z0_ref, z1_ref, z2_ref)
    as_ = (a0_ref, a1_ref, a2_ref)
    scores(i, z0_ref)
    runs = weights(z0_ref, a0_ref, True, tuple(jnp.ones((1, LANES), F32) for _ in range(nsub)))
    scores(i - 1, z1_ref, i - 1 >= 0)

    def body(t, runs):
        for u in range(3):
            k = 3 * t + 1 + u
            cur = (1 + u) % 3
            b_next = i - (k + 1)
            scores(b_next, zs[(cur + 1) % 3], b_next >= 0)
            values(i - (k - 2), as_[(cur + 1) % 3])
            runs = weights(zs[cur], as_[cur], False, runs)
        return runs

    n_steps = (i + 2) // 3
    lax.fori_loop(0, n_steps, body, runs)
    k_last = 3 * n_steps
    values(i - (k_last - 1), a2_ref)
    values(i - k_last, a0_ref)
    o_ref[...] = (acc_ref[...].T * _silu(gate_ref[...])).astype(o_ref.dtype)


def _sb_prompt(bias, proj, q_sec, z_sec, kp, vt, bq):
    _, t, _ = proj.shape
    assert t % bq == 0 and bq % LANES == 0
    return pl.pallas_call(
        functools.partial(_sb_body, bq=bq),
        grid=(H_B, t // bq),
        in_specs=[
            pl.BlockSpec(memory_space=pltpu.SMEM),
            pl.BlockSpec((None, bq, D_B), lambda h, i: (q_sec, i, h)),
            pl.BlockSpec((None, bq, D_B), lambda h, i: (z_sec, i, h)),
            pl.BlockSpec((None, t, D_B), lambda h, i: (h, 0, 0)),
            pl.BlockSpec((None, t // bq, D_B, bq), lambda h, i: (h, 0, 0, 0)),
        ],
        out_specs=pl.BlockSpec((bq, D_B), lambda h, i: (i, h)),
        out_shape=jax.ShapeDtypeStruct((t, H_B * D_B), BF16),
        scratch_shapes=[pltpu.VMEM((D_B, bq), F32)]
        + [pltpu.VMEM((bq, bq), F32)] * 3 + [pltpu.VMEM((bq, bq), BF16)] * 3,
        compiler_params=_cparams(("parallel", "arbitrary")),
        name="sb_prompt",
    )(bias, proj, proj, kp, vt)


def _sb_key_layouts(k, v, bq):
    t = k.shape[0]
    nb = t // LANES

    def interleave(a):
        a = a.astype(BF16).reshape(nb, SUBLANES, KEY_SEG, H_B, D_B)
        return jnp.transpose(a, (3, 0, 2, 1, 4))

    kp = interleave(k).reshape(H_B, t, D_B)
    vt = jnp.swapaxes(interleave(v).reshape(H_B, t // bq, bq, D_B), 2, 3)
    return kp, vt


def _conv_rows(xp_ref, w_ref, r0, rows, lo):
    acc = None
    for i in range(CONV_W):
        start = SUBLANES - (CONV_W - 1) + i + r0
        term = xp_ref[start:start + rows, lo:lo + LANES] * w_ref[i:i + 1, lo:lo + LANES]
        acc = term if acc is None else acc + term
    return acc


def _split_bf16(a):
    hi = a.astype(BF16)
    return hi, (a - hi.astype(F32)).astype(BF16)


def _dot_split(a, b):
    ah, al = a
    bh, bl = b
    return _dot(ah, bh) + (_dot(ah, bl) + _dot(al, bh))


def _tri_inverse_unit_lower(lmats, n):
    row = lax.broadcasted_iota(jnp.int32, (n, n), 0)
    col = lax.broadcasted_iota(jnp.int32, (n, n), 1)
    eye = jnp.where(row == col, 1.0, 0.0).astype(F32)
    ms = [-l for l in lmats]
    invs = [eye + m for m in ms]
    k = 2
    while k < n:
        splits = [_split_bf16(m) for m in ms]
        ms = [_dot_split(s, s) for s in splits]
        invs = [_dot_split(_split_bf16(iv), _split_bf16(eye + m)) for iv, m in zip(invs, ms)]
        k *= 2
    return invs


def _gdn_body(q_ref, k_ref, v_ref, z_ref, ab_ref, cw_ref, alog_ref, dtb_ref, gn_ref,
              o_ref, s_out_ref, tail_ref, xp_ref, s_ref, *, cpb):
    c = GDN_CHUNK
    rows = cpb * c
    step = pl.program_id(0)

    @pl.when(step == 0)
    def _():
        xp_ref[0:SUBLANES, :] = jnp.zeros((SUBLANES, CONV_DIM_A), F32)
        s_ref[...] = jnp.zeros_like(s_ref)

    xp_ref[SUBLANES:, 0:QK_A] = q_ref[...]
    xp_ref[SUBLANES:, QK_A:2 * QK_A] = k_ref[...]
    xp_ref[SUBLANES:, 2 * QK_A:] = v_ref[...]

    ab = ab_ref[...]
    g_all = -jnp.exp(alog_ref[...]) * _softplus(ab + dtb_ref[...])
    beta_all = _sigmoid(ab)
    ri = lax.broadcasted_iota(jnp.int32, (c, c), 0)
    ci = lax.broadcasted_iota(jnp.int32, (c, c), 1)
    tril = ri >= ci
    strict = ri > ci
    tri_f = jnp.where(tril, 1.0, 0.0).astype(F32)

    items = []
    for cc in range(cpb):
        r0 = cc * c
        gc = _dot_hi(tri_f, g_all[r0:r0 + c])
        gct = gc.T
        beta_c = beta_all[r0:r0 + c]
        for h in range(H_A):
            lo = h * DK_A
            qh = _silu(_conv_rows(xp_ref, cw_ref, r0, c, lo))
            kh = _silu(_conv_rows(xp_ref, cw_ref, r0, c, QK_A + lo))
            vh = _silu(_conv_rows(xp_ref, cw_ref, r0, c, 2 * QK_A + lo))
            qh = qh * lax.rsqrt(jnp.sum(qh * qh, axis=-1, keepdims=True) + 1e-6) * (DK_A ** -0.5)
            kh = kh * lax.rsqrt(jnp.sum(kh * kh, axis=-1, keepdims=True) + 1e-6)
            gcol = gc[:, h:h + 1]
            grow = gct[h:h + 1, :]
            beta = beta_c[:, H_A + h:H_A + h + 1]
            decay = jnp.where(tril, jnp.exp(jnp.where(tril, gcol - grow, 0.0)), 0.0)
            kb = kh.astype(BF16)
            lmat = jnp.where(strict, beta * _dot_nt(kb, kb) * decay, 0.0)
            egc = jnp.exp(gcol)
            glast = gcol[c - 1:c]
            items.append(dict(
                r0=r0, h=h, lmat=lmat,
                rhs=jnp.concatenate([vh * beta, kh * (beta * egc)], axis=-1),
                qk=jnp.where(tril, _dot_nt(qh.astype(BF16), kb) * decay, 0.0).astype(BF16),
                qg=(qh * egc).astype(BF16),
                kdec_t=(kh * jnp.exp(glast - gcol)).T.astype(BF16),
                sdecay=jnp.exp(glast)))
    invs = _tri_inverse_unit_lower([it["lmat"] for it in items], c)
    sols = [_dot_split(_split_bf16(iv), _split_bf16(it["rhs"])) for iv, it in zip(invs, items)]

    for it, sol in zip(items, sols):
        r0, h = it["r0"], it["h"]
        lo = h * DV_A
        s_old = s_ref[h]
        sb = s_old.astype(BF16)
        v_new = sol[:, :DV_A] - _dot(sol[:, DV_A:].astype(BF16), sb)
        vnb = v_new.astype(BF16)
        o = _dot(it["qg"], sb) + _dot(it["qk"], vnb)
        s_ref[h] = s_old * it["sdecay"] + _dot(it["kdec_t"], vnb)
        o = _rms(o, gn_ref[...]) * _silu(z_ref[r0:r0 + c, lo:lo + DV_A])
        o_ref[r0:r0 + c, lo:lo + DV_A] = o.astype(o_ref.dtype)

    xp_ref[0:SUBLANES, :] = xp_ref[rows:rows + SUBLANES, :]

    @pl.when(step == pl.num_programs(0) - 1)
    def _():
        s_out_ref[...] = s_ref[...]
        tail_ref[...] = xp_ref[0:SUBLANES, :]


def _gdn_prompt(proj, ab, conv_w, a_log_row, dt_bias_row, g_norm, cpb):
    _, t, _ = proj.shape
    rows = cpb * GDN_CHUNK
    assert t % rows == 0
    sec = lambda s: pl.BlockSpec((None, rows, SEC), lambda i: (s, i, 0))
    full = lambda shape: pl.BlockSpec(shape, lambda i: (0,) * len(shape))
    return pl.pallas_call(
        functools.partial(_gdn_body, cpb=cpb),
        grid=(t // rows,),
        in_specs=[sec(0), sec(1), sec(2), sec(3),
                  pl.BlockSpec((rows, LANES), lambda i: (i, 0)),
                  full((CONV_W, CONV_DIM_A)), full((1, LANES)), full((1, LANES)), full((1, DV_A))],
        out_specs=[pl.BlockSpec((rows, V_A), lambda i: (i, 0)),
                   full((H_A, DK_A, DV_A)), full((SUBLANES, CONV_DIM_A))],
        out_shape=[jax.ShapeDtypeStruct((t, V_A), BF16),
                   jax.ShapeDtypeStruct((H_A, DK_A, DV_A), F32),
                   jax.ShapeDtypeStruct((SUBLANES, CONV_DIM_A), F32)],
        scratch_shapes=[pltpu.VMEM((rows + SUBLANES, CONV_DIM_A), F32),
                        pltpu.VMEM((H_A, DK_A, DV_A), F32)],
        compiler_params=_cparams(("arbitrary",)),
        name="gdn_prompt",
    )(proj, proj, proj, proj, ab, conv_w, a_log_row, dt_bias_row, g_norm)


def _ssd_body(z0_ref, z1_ref, x0_ref, x1_ref, bc_ref, dt_ref, cw_ref, cb_ref, alog_ref, dtb_ref,
              dskip_ref, gn_ref, exp_ref, o_ref, s_out_ref, tail_ref, xp_ref, s_ref, *, cpb):
    c = SSD_CHUNK
    rows = cpb * c
    step = pl.program_id(0)
    gw = D_INNER_C // N_GROUPS_C
    n_bc = N_GROUPS_C * D_STATE_C

    @pl.when(step == 0)
    def _():
        xp_ref[0:SUBLANES, :] = jnp.zeros((SUBLANES, CONV_DIM_C), F32)
        s_ref[...] = jnp.zeros_like(s_ref)

    xp_ref[SUBLANES:, 0:SEC] = x0_ref[...]
    xp_ref[SUBLANES:, SEC:2 * SEC] = x1_ref[...]
    xp_ref[SUBLANES:, 2 * SEC:] = bc_ref[...]

    dt_all = _softplus(dt_ref[...] + dtb_ref[...])
    da_all = dt_all * (-jnp.exp(alog_ref[...]))
    ri = lax.broadcasted_iota(jnp.int32, (c, c), 0)
    ci = lax.broadcasted_iota(jnp.int32, (c, c), 1)
    tril = ri >= ci
    tri_f = jnp.where(tril, 1.0, 0.0).astype(F32)
    lane = lax.broadcasted_iota(jnp.int32, (1, LANES), 1)
    left = lane < HEAD_DIM_C
    rowi = lax.broadcasted_iota(jnp.int32, (LANES, 1), 0)
    top = rowi < HEAD_DIM_C
    expand = exp_ref[...]

    for cc in range(cpb):
        r0 = cc * c
        dt = dt_all[r0:r0 + c]
        acum = _dot_hi(tri_f, da_all[r0:r0 + c])
        acum_t = acum.T
        dt_x = _select_dot(dt, expand)
        ac_x = _select_dot(acum, expand)
        for g in range(N_GROUPS_C):
            bm = _silu(_conv_rows(xp_ref, cw_ref, r0, c, D_INNER_C + g * D_STATE_C)
                       + cb_ref[:, D_INNER_C + g * D_STATE_C:D_INNER_C + (g + 1) * D_STATE_C])
            cm = _silu(_conv_rows(xp_ref, cw_ref, r0, c, D_INNER_C + n_bc + g * D_STATE_C)
                       + cb_ref[:, D_INNER_C + n_bc + g * D_STATE_C:D_INNER_C + n_bc + (g + 1) * D_STATE_C])
            bmb = bm.astype(BF16)
            cmb = cm.astype(BF16)
            cb = _dot_nt(cmb, bmb)
            ys = []
            for pr in range(gw // LANES):
                lo = g * gw + pr * LANES
                h0 = lo // HEAD_DIM_C
                xs = _silu(_conv_rows(xp_ref, cw_ref, r0, c, lo) + cb_ref[:, lo:lo + LANES])
                xdt = xs * dt_x[:, lo:lo + LANES]
                acx = ac_x[:, lo:lo + LANES]
                y = None
                for hh in range(2):
                    hd = h0 + hh
                    seg = acum[:, hd:hd + 1] - acum_t[hd:hd + 1, :]
                    lm = jnp.where(tril, jnp.exp(jnp.where(tril, seg, 0.0)), 0.0)
                    xm = jnp.where(left if hh == 0 else jnp.logical_not(left), xdt, 0.0)
                    term = _dot((cb * lm).astype(BF16), xm.astype(BF16))
                    y = term if y is None else y + term
                s_old = s_ref[lo:lo + LANES, :]
                y = y + _dot_nt(cmb, s_old.astype(BF16)) * jnp.exp(acx)
                dte = jnp.exp(acx[c - 1:c] - acx)
                st = _dot((xdt * dte).T.astype(BF16), bmb)
                al0 = acum[c - 1:c, h0:h0 + 1]
                al1 = acum[c - 1:c, h0 + 1:h0 + 2]
                s_ref[lo:lo + LANES, :] = s_old * jnp.exp(jnp.where(top, al0, al1)) + st
                y = y + xs * dskip_ref[:, lo:lo + LANES]
                zz = (z0_ref if lo < SEC else z1_ref)[r0:r0 + c, (lo % SEC):(lo % SEC) + LANES]
                ys.append(y * _silu(zz))
            yg = jnp.concatenate(ys, axis=-1)
            yg = _rms(yg, gn_ref[:, g * gw:(g + 1) * gw])
            o_ref[r0:r0 + c, g * gw:(g + 1) * gw] = yg.astype(o_ref.dtype)

    xp_ref[0:SUBLANES, :] = xp_ref[rows:rows + SUBLANES, :]

    @pl.when(step == pl.num_programs(0) - 1)
    def _():
        s_out_ref[...] = s_ref[...]
        tail_ref[...] = xp_ref[0:SUBLANES, :]


def _ssd_prompt(proj, dt_raw, conv_w, conv_b, a_log_row, dt_bias_row, dskip_x, g_norm, expand, cpb):
    _, t, _ = proj.shape
    rows = cpb * SSD_CHUNK
    assert t % rows == 0
    sec = lambda s: pl.BlockSpec((None, rows, SEC), lambda i: (s, i, 0))
    full = lambda shape: pl.BlockSpec(shape, lambda i: (0,) * len(shape))
    return pl.pallas_call(
        functools.partial(_ssd_body, cpb=cpb),
        grid=(t // rows,),
        in_specs=[sec(0), sec(1), sec(2), sec(3), sec(4),
                  pl.BlockSpec((rows, LANES), lambda i: (i, 0)),
                  full((CONV_W, CONV_DIM_C)), full((1, CONV_DIM_C)), full((1, LANES)), full((1, LANES)),
                  full((1, D_INNER_C)), full((1, D_INNER_C)), full((LANES, D_INNER_C))],
        out_specs=[pl.BlockSpec((rows, D_INNER_C), lambda i: (i, 0)),
                   full((D_INNER_C, D_STATE_C)), full((SUBLANES, CONV_DIM_C))],
        out_shape=[jax.ShapeDtypeStruct((t, D_INNER_C), BF16),
                   jax.ShapeDtypeStruct((D_INNER_C, D_STATE_C), F32),
                   jax.ShapeDtypeStruct((SUBLANES, CONV_DIM_C), F32)],
        scratch_shapes=[pltpu.VMEM((rows + SUBLANES, CONV_DIM_C), F32),
                        pltpu.VMEM((D_INNER_C, D_STATE_C), F32)],
        compiler_params=_cparams(("arbitrary",)),
        name="ssd_prompt",
    )(proj, proj, proj, proj, proj, dt_raw, conv_w, conv_b, a_log_row, dt_bias_row, dskip_x, g_norm, expand)


def _head_expand():
    r = jnp.arange(LANES)[:, None]
    cidx = jnp.arange(D_INNER_C)[None, :]
    return (cidx // HEAD_DIM_C == r).astype(BF16)


def _row_to_col(row, eye):
    return jnp.sum(jnp.where(eye, row, 0.0), axis=1, keepdims=True)


def _col_to_row(col, eye):
    return jnp.sum(jnp.where(eye, col, 0.0), axis=0, keepdims=True)


def _eye(n):
    return lax.broadcasted_iota(jnp.int32, (n, n), 0) == lax.broadcasted_iota(jnp.int32, (n, n), 1)


def _conv_step(x_row, buf_ref, w_ref, lo, hi):
    acc = x_row * w_ref[CONV_W - 1:CONV_W, lo:hi]
    for i in range(CONV_W - 1):
        acc = acc + buf_ref[i:i + 1, lo:hi] * w_ref[i:i + 1, lo:hi]
    return acc


def _gdn_step_body(q_ref, k_ref, v_ref, z_ref, ab_ref, buf_ref, s_ref, cw_ref, alog_ref, dtb_ref, gn_ref,
                   o_ref, s_out_ref, buf_out_ref):
    eye = _eye(LANES)
    ab = ab_ref[...]
    eg_all = jnp.exp(-jnp.exp(alog_ref[...]) * _softplus(ab + dtb_ref[...]))
    beta_all = _sigmoid(ab)
    buf_out_ref[0:CONV_W - 2, :] = buf_ref[1:CONV_W - 1, :]
    pre = (q_ref, k_ref, v_ref)
    for n in range(3):
        buf_out_ref[CONV_W - 2:CONV_W - 1, n * SEC:(n + 1) * SEC] = pre[n][...]
    for h in range(H_A):
        lo = h * DK_A
        q = _silu(_conv_step(q_ref[:, lo:lo + DK_A], buf_ref, cw_ref, lo, lo + DK_A))
        k = _silu(_conv_step(k_ref[:, lo:lo + DK_A], buf_ref, cw_ref, QK_A + lo, QK_A + lo + DK_A))
        v = _silu(_conv_step(v_ref[:, lo:lo + DV_A], buf_ref, cw_ref, 2 * QK_A + lo, 2 * QK_A + lo + DV_A))
        q = q * lax.rsqrt(jnp.sum(q * q, axis=-1, keepdims=True) + 1e-6) * (DK_A ** -0.5)
        k = k * lax.rsqrt(jnp.sum(k * k, axis=-1, keepdims=True) + 1e-6)
        eg = eg_all[:, h:h + 1]
        beta = beta_all[:, H_A + h:H_A + h + 1]
        s0 = s_ref[h]
        kc = _row_to_col(k, eye)
        qc = _row_to_col(q, eye)
        sk = jnp.sum(s0 * kc, axis=0, keepdims=True)
        sq = jnp.sum(s0 * qc, axis=0, keepdims=True)
        v_new = beta * (v - eg * sk)
        o = eg * sq + jnp.sum(q * k, axis=-1, keepdims=True) * v_new
        s_out_ref[h] = s0 * eg + kc * v_new
        o_ref[:, lo:lo + DV_A] = _rms(o, gn_ref[...]) * _silu(z_ref[:, lo:lo + DV_A])


def _gdn_step(proj, ab, conv_buf, state, conv_w, a_log_row, dt_bias_row, g_norm):
    b = ab.shape[0]
    sec = lambda s: pl.BlockSpec((None, None, 1, SEC), lambda i: (s, i, 0, 0))
    full = lambda shape: pl.BlockSpec(shape, lambda i: (0,) * len(shape))
    return pl.pallas_call(
        _gdn_step_body,
        grid=(b,),
        in_specs=[sec(0), sec(1), sec(2), sec(3),
                  pl.BlockSpec((None, 1, LANES), lambda i: (i, 0, 0)),
                  pl.BlockSpec((None, CONV_W - 1, CONV_DIM_A), lambda i: (i, 0, 0)),
                  pl.BlockSpec((None, H_A, DK_A, DV_A), lambda i: (i, 0, 0, 0)),
                  full((CONV_W, CONV_DIM_A)), full((1, LANES)), full((1, LANES)), full((1, DV_A))],
        out_specs=[pl.BlockSpec((None, 1, V_A), lambda i: (i, 0, 0)),
                   pl.BlockSpec((None, H_A, DK_A, DV_A), lambda i: (i, 0, 0, 0)),
                   pl.BlockSpec((None, CONV_W - 1, CONV_DIM_A), lambda i: (i, 0, 0))],
        out_shape=[jax.ShapeDtypeStruct((b, 1, V_A), F32),
                   jax.ShapeDtypeStruct((b, H_A, DK_A, DV_A), F32),
                   jax.ShapeDtypeStruct((b, CONV_W - 1, CONV_DIM_A), F32)],
        compiler_params=_cparams(("parallel",)),
        name="gdn_step",
    )(proj, proj, proj, proj, ab, conv_buf, state, conv_w, a_log_row, dt_bias_row, g_norm)


def _ssd_step_body(z0_ref, z1_ref, x0_ref, x1_ref, bc_ref, dt_ref, buf_ref, s_ref, cw_ref, cb_ref, alog_ref,
                   dtb_ref, dskip_ref, gn_ref, o_ref, s_out_ref, buf_out_ref):
    eye = _eye(LANES)
    gw = D_INNER_C // N_GROUPS_C
    n_bc = N_GROUPS_C * D_STATE_C
    top = lax.broadcasted_iota(jnp.int32, (LANES, 1), 0) < HEAD_DIM_C
    dt_all = _softplus(dt_ref[...] + dtb_ref[...])
    dec_all = jnp.exp(dt_all * (-jnp.exp(alog_ref[...])))
    buf_out_ref[0:CONV_W - 2, :] = buf_ref[1:CONV_W - 1, :]
    pre = (x0_ref, x1_ref, bc_ref)
    for n in range(3):
        buf_out_ref[CONV_W - 2:CONV_W - 1, n * SEC:(n + 1) * SEC] = pre[n][...]

    def conv(ref, lo_in, lo):
        return _silu(_conv_step(ref[:, lo_in:lo_in + LANES], buf_ref, cw_ref, lo, lo + LANES)
                     + cb_ref[:, lo:lo + LANES])

    for g in range(N_GROUPS_C):
        bm = conv(bc_ref, g * D_STATE_C, D_INNER_C + g * D_STATE_C)
        cm = conv(bc_ref, n_bc + g * D_STATE_C, D_INNER_C + n_bc + g * D_STATE_C)
        ys = []
        for pr in range(gw // LANES):
            lo = g * gw + pr * LANES
            h0 = lo // HEAD_DIM_C
            xs = conv(x0_ref if lo < SEC else x1_ref, lo % SEC, lo)
            xc = _row_to_col(xs, eye)
            dt_c = jnp.where(top, dt_all[:, h0:h0 + 1], dt_all[:, h0 + 1:h0 + 2])
            dec_c = jnp.where(top, dec_all[:, h0:h0 + 1], dec_all[:, h0 + 1:h0 + 2])
            h_new = s_ref[lo:lo + LANES, :] * dec_c + (xc * dt_c) * bm
            s_out_ref[lo:lo + LANES, :] = h_new
            y = _col_to_row(jnp.sum(h_new * cm, axis=1, keepdims=True), eye)
            y = y + xs * dskip_ref[:, lo:lo + LANES]
            zz = (z0_ref if lo < SEC else z1_ref)[:, (lo % SEC):(lo % SEC) + LANES]
            ys.append(y * _silu(zz))
        yg = jnp.concatenate(ys, axis=-1)
        o_ref[:, g * gw:(g + 1) * gw] = _rms(yg, gn_ref[:, g * gw:(g + 1) * gw])


def _ssd_step(proj, dt_raw, conv_buf, state, conv_w, conv_b, a_log_row, dt_bias_row, dskip_x, g_norm):
    b = dt_raw.shape[0]
    sec = lambda s: pl.BlockSpec((None, None, 1, SEC), lambda i: (s, i, 0, 0))
    full = lambda shape: pl.BlockSpec(shape, lambda i: (0,) * len(shape))
    return pl.pallas_call(
        _ssd_step_body,
        grid=(b,),
        in_specs=[sec(0), sec(1), sec(2), sec(3), sec(4),
                  pl.BlockSpec((None, 1, LANES), lambda i: (i, 0, 0)),
                  pl.BlockSpec((None, CONV_W - 1, CONV_DIM_C), lambda i: (i, 0, 0)),
                  pl.BlockSpec((None, D_INNER_C, D_STATE_C), lambda i: (i, 0, 0)),
                  full((CONV_W, CONV_DIM_C)), full((1, CONV_DIM_C)), full((1, LANES)), full((1, LANES)),
                  full((1, D_INNER_C)), full((1, D_INNER_C))],
        out_specs=[pl.BlockSpec((None, 1, D_INNER_C), lambda i: (i, 0, 0)),
                   pl.BlockSpec((None, D_INNER_C, D_STATE_C), lambda i: (i, 0, 0)),
                   pl.BlockSpec((None, CONV_W - 1, CONV_DIM_C), lambda i: (i, 0, 0))],
        out_shape=[jax.ShapeDtypeStruct((b, 1, D_INNER_C), F32),
                   jax.ShapeDtypeStruct((b, D_INNER_C, D_STATE_C), F32),
                   jax.ShapeDtypeStruct((b, CONV_W - 1, CONV_DIM_C), F32)],
        compiler_params=_cparams(("parallel",)),
        name="ssd_step",
    )(proj, proj, proj, proj, proj, dt_raw, conv_buf, state, conv_w, conv_b, a_log_row, dt_bias_row,
      dskip_x, g_norm)


SB_PAGES_PER_STEP = 16


def _sb_paged_body(pt_ref, q_ref, z_ref, bias_ref, *refs, ppb):
    del pt_ref
    k_refs = refs[:ppb]
    v_refs = refs[ppb:2 * ppb]
    o_ref, acc_ref, run_ref = refs[2 * ppb:]
    p = pl.program_id(1)
    grp = PAGE_SIZE // KEY_SEG
    rows = KEY_SEG * H_B

    @pl.when(p == 0)
    def _():
        acc_ref[...] = jnp.zeros_like(acc_ref)
        run_ref[...] = jnp.zeros_like(run_ref)

    sub = lax.broadcasted_iota(jnp.int32, (SUBLANES, LANES), 0)
    lane = lax.broadcasted_iota(jnp.int32, (SUBLANES, LANES), 1)
    own = (lane % H_B) == sub
    ri = lax.broadcasted_iota(jnp.int32, (rows, 2 * rows), 0)
    ci = lax.broadcasted_iota(jnp.int32, (rows, 2 * rows), 1)
    same_head = (ri % H_B) == (ci % H_B)
    tmat = jnp.where(same_head & ((ci >= rows) | (ri // H_B >= ci // H_B)), 1.0, 0.0).astype(BF16)
    qb = q_ref[...].astype(BF16)
    bias = bias_ref[...]
    pages = range(ppb)
    k2 = [k_refs[r][...].reshape(PAGE_SIZE * H_B, D_B).astype(BF16) for r in pages]
    zs = []
    for r in pages:
        zrows = []
        for g in range(grp):
            zt = _dot_nt(qb, k2[r][g * rows:(g + 1) * rows])
            zrows.append(jnp.sum(jnp.where(own, zt, 0.0), axis=0, keepdims=True))
        zs.append(jnp.concatenate(zrows, axis=0) * (D_B ** -0.5) + bias)
    sps = [_split_bf16(_softplus(z)) for z in zs]
    css = [_dot(hi, tmat) + _dot(lo, tmat) for hi, lo in sps]
    offs = []
    tots = []
    for cs in css:
        inc = cs[:, rows:]
        for d in (1, 2, 4):
            sh = pltpu.roll(inc, SUBLANES - d, axis=0)
            inc = inc + jnp.where(sub + d < SUBLANES, sh, 0.0)
        offs.append(jnp.where(sub < SUBLANES - 1, pltpu.roll(inc, SUBLANES - 1, axis=0), 0.0))
        tots.append(inc[0:1])
    run_after = run_ref[...]
    acc = acc_ref[...]
    for r in pages:
        a = jnp.exp(zs[r] - (css[r][:, :rows] + (offs[r] + run_after)))
        lhs = jnp.concatenate([jnp.where(own, a[g:g + 1], 0.0) for g in range(grp)], axis=1)
        v2 = v_refs[r][...].reshape(PAGE_SIZE * H_B, D_B).astype(BF16)
        acc = acc + _dot(lhs.astype(BF16), v2)
        run_after = run_after + tots[r]
    acc_ref[...] = acc
    run_ref[...] = run_after

    @pl.when(p == pl.num_programs(1) - 1)
    def _():
        o_ref[...] = acc * _silu(z_ref[...])


def _sb_paged(page_table, q, zgate, bias_row, cache_k, cache_v, layer):
    b, npg = page_table.shape
    ppb = SB_PAGES_PER_STEP
    assert npg % ppb == 0

    def page_spec(r):
        return pl.BlockSpec((None, None, PAGE_SIZE, H_B, D_B),
                            lambda i, p, pt: (layer, pt[i, npg - 1 - (p * ppb + r)], 0, 0, 0))

    row = pl.BlockSpec((None, H_B, D_B), lambda i, p, pt: (i, 0, 0))
    grid_spec = pltpu.PrefetchScalarGridSpec(
        num_scalar_prefetch=1,
        grid=(b, npg // ppb),
        in_specs=[row, row, pl.BlockSpec((1, LANES), lambda i, p, pt: (0, 0))]
        + [page_spec(r) for r in range(ppb)] * 2,
        out_specs=row,
        scratch_shapes=[pltpu.VMEM((H_B, D_B), F32), pltpu.VMEM((1, LANES), F32)],
    )
    return pl.pallas_call(
        functools.partial(_sb_paged_body, ppb=ppb),
        grid_spec=grid_spec,
        out_shape=jax.ShapeDtypeStruct((b, H_B, D_B), F32),
        compiler_params=_cparams(("parallel", "arbitrary")),
        name="sb_paged",
    )(page_table, q, zgate, bias_row, *([cache_k] * ppb), *([cache_v] * ppb))


def _pad_lanes(v):
    return jnp.pad(v.astype(F32), (0, LANES - v.shape[0])).reshape(1, LANES)


def _even_weights(w_in):
    o_ab = CONV_DIM_A
    o_z = o_ab + 2 * H_A
    w_main = jnp.concatenate([w_in[:, :o_ab], w_in[:, o_z:]], axis=1).astype(BF16)
    w_sec = jnp.transpose(w_main.reshape(D_MODEL, 8, SEC), (1, 0, 2))
    w_small = jnp.pad(w_in[:, o_ab:o_z], ((0, 0), (0, LANES - 2 * H_A))).astype(BF16)
    return w_sec, w_small


def _odd_weights(w_in):
    n_main = D_INNER_C + CONV_DIM_C
    w_sec = jnp.transpose(w_in[:, :n_main].astype(BF16).reshape(D_MODEL, 5, SEC), (1, 0, 2))
    w_small = jnp.pad(w_in[:, n_main:], ((0, 0), (0, LANES - H_C))).astype(BF16)
    return w_sec, w_small


def kernel(x_prompt, x_sample, cache_sb_k, cache_sb_v, state_gdn, state_gdn_conv, state_ssd, state_ssd_conv, page_table, even_norm_pre, even_norm_post, even_w_in, gdn_conv_w, gdn_a_log, gdn_dt_bias, gdn_norm_w, sb_bias, even_w_out, odd_norm_pre, odd_norm_post, odd_w_in, ssd_conv_w, ssd_conv_b, ssd_a_log, ssd_dt_bias, ssd_d, ssd_norm_w, odd_w_out):
    t = x_prompt.shape[1]
    bs = x_sample.shape[0]
    xp = x_prompt[0]
    xs = x_sample[:, 0]
    tail = SUBLANES - (CONV_W - 1)
    tm_p = 1024
    tm_s = bs

    w_sec, w_small = _even_weights(even_w_in[0])
    w_out_e = even_w_out[0].astype(BF16)
    a_log_e = _pad_lanes(gdn_a_log[0])
    dt_bias_e = _pad_lanes(gdn_dt_bias[0])
    proj, ab, k_b, v_b = _proj(xp, even_norm_pre, w_sec, w_small, tm_p, own=(5, 6))
    sb_k_prompt = k_b.reshape(1, 1, t, H_B, D_B)
    sb_v_prompt = v_b.reshape(1, 1, t, H_B, D_B)
    kp, vt = _sb_key_layouts(k_b, v_b, SB_BQ)
    o_b = _sb_prompt(sb_bias[0], proj, 4, 5, kp, vt, SB_BQ)
    o_a, gdn_s, gdn_tail = _gdn_prompt(proj, ab, gdn_conv_w[0], a_log_e, dt_bias_e, gdn_norm_w, 4)
    y1 = _outproj(o_a, o_b, w_out_e, even_norm_post, xp, tm_p)
    proj_s, ab_s, k_b_s, v_b_s = _proj(xs, even_norm_pre, w_sec, w_small, tm_s, own=(5, 6))
    sb_k_sample = k_b_s.reshape(1, bs, 1, H_B, D_B)
    sb_v_sample = v_b_s.reshape(1, bs, 1, H_B, D_B)
    o_b_s = _sb_paged(page_table, proj_s[4].reshape(bs, H_B, D_B), proj_s[5].reshape(bs, H_B, D_B),
                      jnp.tile(sb_bias[0].astype(F32), LANES // H_B).reshape(1, LANES),
                      cache_sb_k, cache_sb_v, 0)
    o_a_s, gdn_s_s, gdn_conv_s = _gdn_step(proj_s.reshape(6, bs, 1, SEC), ab_s.reshape(bs, 1, LANES),
                                           state_gdn_conv[0], state_gdn[0], gdn_conv_w[0], a_log_e,
                                           dt_bias_e, gdn_norm_w)
    y1_s = _outproj(o_a_s.reshape(bs, V_A), o_b_s.reshape(bs, W_B), w_out_e, even_norm_post, xs, tm_s)

    w_sec_o, w_small_o = _odd_weights(odd_w_in[0])
    w_out_o = odd_w_out[0].astype(BF16)
    a_log_o = _pad_lanes(ssd_a_log[0])
    dt_bias_o = _pad_lanes(ssd_dt_bias[0])
    dskip_x = jnp.repeat(ssd_d[0].astype(F32), HEAD_DIM_C).reshape(1, D_INNER_C)
    proj_o, dt_raw = _proj(y1, odd_norm_pre, w_sec_o, w_small_o, tm_p)
    y_ssd, ssd_s, ssd_tail = _ssd_prompt(proj_o, dt_raw, ssd_conv_w[0], ssd_conv_b, a_log_o, dt_bias_o,
                                         dskip_x, ssd_norm_w, _head_expand(), 2)
    y2 = _outproj(y_ssd, y_ssd, w_out_o, odd_norm_post, y1, tm_p, 0, 1)
    proj_os, dt_raw_s = _proj(y1_s, odd_norm_pre, w_sec_o, w_small_o, tm_s)
    y_ssd_s, ssd_s_s, ssd_conv_s = _ssd_step(proj_os.reshape(5, bs, 1, SEC), dt_raw_s.reshape(bs, 1, LANES),
                                             state_ssd_conv[0], state_ssd[0].reshape(bs, D_INNER_C, D_STATE_C),
                                             ssd_conv_w[0], ssd_conv_b, a_log_o, dt_bias_o, dskip_x, ssd_norm_w)
    y_ssd_s = y_ssd_s.reshape(bs, D_INNER_C)
    y2_s = _outproj(y_ssd_s, y_ssd_s, w_out_o, odd_norm_post, y1_s, tm_s, 0, 1)

    return (y2[None], y2_s[:, None], sb_k_prompt, sb_v_prompt, sb_k_sample, sb_v_sample,
            gdn_s[None, None], gdn_s_s[None], gdn_tail[tail:][None, None], gdn_conv_s[None],
            ssd_s.reshape(1, 1, H_C, HEAD_DIM_C, D_STATE_C),
            ssd_s_s.reshape(1, bs, H_C, HEAD_DIM_C, D_STATE_C),
            ssd_tail[tail:][None, None], ssd_conv_s[None])
```

```python
import functools

import jax
import jax.numpy as jnp
from jax import lax
from jax.experimental import pallas as pl
from jax.experimental.pallas import tpu as pltpu

F32 = jnp.float32
BF16 = jnp.bfloat16
HIGHEST = lax.Precision.HIGHEST

LANES = 128
SUBLANES = 8
VMEM_LIMIT_BYTES = 48 * 1024 * 1024

D_MODEL = 1024
H_A = 8
DK_A = 128
DV_A = 128
H_B = 8
D_B = 128
CONV_W = 4
GDN_CHUNK = 64
QK_A = H_A * DK_A
V_A = H_A * DV_A
W_B = H_B * D_B
CONV_DIM_A = 2 * QK_A + V_A
D_INNER_C = 2 * D_MODEL
HEAD_DIM_C = 64
H_C = D_INNER_C // HEAD_DIM_C
N_GROUPS_C = 4
D_STATE_C = 128
SSD_CHUNK = 64
CONV_DIM_C = D_INNER_C + 2 * N_GROUPS_C * D_STATE_C
PAGE_SIZE = 128
RMS_EPS = 1e-6
SEC = 1024


def _cparams(sem):
    return pltpu.CompilerParams(dimension_semantics=sem, vmem_limit_bytes=VMEM_LIMIT_BYTES)


def _dot(a, b):
    return jnp.dot(a, b, preferred_element_type=F32)


def _dot_nt(a, b):
    return lax.dot_general(a, b, (((1,), (1,)), ((), ())), preferred_element_type=F32)


def _dot_hi(a, b):
    return jnp.dot(a, b, preferred_element_type=F32, precision=HIGHEST)


def _split3_bf16(x):
    hi = x.astype(BF16)
    r1 = x - hi.astype(F32)
    mid = r1.astype(BF16)
    return hi, mid, (r1 - mid.astype(F32)).astype(BF16)


def _select_dot(x, sel):
    hi, mid, lo = _split3_bf16(x)
    return _dot(hi, sel) + (_dot(mid, sel) + _dot(lo, sel))


def _sigmoid(x):
    return 1.0 / (1.0 + jnp.exp(-x))


def _silu(x):
    return x * _sigmoid(x)


def _softplus(x):
    return jnp.maximum(x, 0.0) + jnp.log1p(jnp.exp(-jnp.abs(x)))


def _rms(x, w):
    return x * lax.rsqrt(jnp.mean(x * x, axis=-1, keepdims=True) + RMS_EPS) * w


def _proj_body(x_ref, nw_ref, w_ref, ws_ref, o_ref, os_ref, *rest, own):
    own_refs, h_ref = rest[:-1], rest[-1]
    j = pl.program_id(1)

    @pl.when(j == 0)
    def _():
        hb = _rms(x_ref[...], nw_ref[...]).astype(BF16)
        h_ref[...] = hb
        os_ref[...] = _dot(hb, ws_ref[...])

    res = _dot(h_ref[...], w_ref[...])
    stacked = None
    for sec, ref in zip(own, own_refs):
        @pl.when(j == sec)
        def _(ref=ref):
            ref[...] = res

        stacked = (j != sec) if stacked is None else stacked & (j != sec)
    if stacked is None:
        o_ref[...] = res
    else:
        @pl.when(stacked)
        def _():
            o_ref[...] = res


def _proj(x, nw, w_sec, w_small, tm, own=()):
    m, d = x.shape
    s = w_sec.shape[0]
    assert m % tm == 0

    def stacked_index(j):
        return j - sum((j >= e).astype(jnp.int32) for e in own) if own else j

    own_spec = pl.BlockSpec((tm, SEC), lambda i, j: (i, 0))
    return pl.pallas_call(
        functools.partial(_proj_body, own=own),
        grid=(m // tm, s),
        in_specs=[
            pl.BlockSpec((tm, d), lambda i, j: (i, 0)),
            pl.BlockSpec((1, d), lambda i, j: (0, 0)),
            pl.BlockSpec((None, d, SEC), lambda i, j: (j, 0, 0)),
            pl.BlockSpec((d, LANES), lambda i, j: (0, 0)),
        ],
        out_specs=[
            pl.BlockSpec((None, tm, SEC), lambda i, j: (stacked_index(j), i, 0)),
            pl.BlockSpec((tm, LANES), lambda i, j: (i, 0)),
        ] + [own_spec] * len(own),
        out_shape=[
            jax.ShapeDtypeStruct((s - len(own), m, SEC), F32),
            jax.ShapeDtypeStruct((m, LANES), F32),
        ] + [jax.ShapeDtypeStruct((m, SEC), F32)] * len(own),
        scratch_shapes=[pltpu.VMEM((tm, d), BF16)],
        compiler_params=_cparams(("parallel", "arbitrary")),
        name="proj",
    )(x, nw, w_sec, w_small)


def _outproj_body(ya_ref, yb_ref, wa_ref, wb_ref, nw_ref, x_ref, o_ref):
    acc = _dot(ya_ref[...].astype(BF16), wa_ref[...]) + _dot(yb_ref[...].astype(BF16), wb_ref[...])
    o_ref[...] = x_ref[...] + _rms(acc, nw_ref[...])


def _outproj(ya, yb, w, nw, x, tm, ya_col=0, yb_col=0):
    m, d = x.shape
    assert m % tm == 0
    return pl.pallas_call(
        _outproj_body,
        grid=(m // tm,),
        in_specs=[
            pl.BlockSpec((tm, SEC), lambda i: (i, ya_col)),
            pl.BlockSpec((tm, SEC), lambda i: (i, yb_col)),
            pl.BlockSpec((SEC, d), lambda i: (0, 0)),
            pl.BlockSpec((SEC, d), lambda i: (1, 0)),
            pl.BlockSpec((1, d), lambda i: (0, 0)),
            pl.BlockSpec((tm, d), lambda i: (i, 0)),
        ],
        out_specs=pl.BlockSpec((tm, d), lambda i: (i, 0)),
        out_shape=jax.ShapeDtypeStruct((m, d), F32),
        compiler_params=_cparams(("parallel",)),
        name="outproj",
    )(ya, yb, w, w, nw, x)


KEY_SEG = LANES // SUBLANES
LOG2E = 1.4426950408889634
SB_BQ = 512
SB_Z2_MAX = 64.0


def _sb_body(bias_ref, q_ref, gate_ref, kp_ref, vt_ref, o_ref, acc_ref, z0_ref, z1_ref, *, bq):
    h = pl.program_id(0)
    i = pl.program_id(1)
    nsub = bq // LANES
    qt = (q_ref[...] * (D_B ** -0.5 * LOG2E)).T.astype(BF16)
    bias2 = bias_ref[h] * LOG2E
    seg = lax.broadcasted_iota(jnp.int32, (SUBLANES, LANES), 0)
    lane = lax.broadcasted_iota(jnp.int32, (SUBLANES, LANES), 1)
    acc_ref[...] = jnp.zeros_like(acc_ref)

    def chain(zt, rel, run):
        part = []
        cs = None
        for r in range(KEY_SEG - 1, -1, -1):
            w = jnp.exp2(jnp.minimum(zt[r * SUBLANES:(r + 1) * SUBLANES], SB_Z2_MAX))
            p = 1.0 / (1.0 + w)
            if rel is not None:
                m = (rel + seg * KEY_SEG + r) < lane
                w = jnp.where(m, w, 0.0)
                p = jnp.where(m, p, 1.0)
            cs = p if cs is None else p * cs
            part.append(w * cs)
        inc = cs
        for d in (1, 2, 4):
            sh = pltpu.roll(inc, SUBLANES - d, axis=0)
            inc = inc * jnp.where(seg + d < SUBLANES, sh, 1.0)
        off = jnp.where(seg < SUBLANES - 1, pltpu.roll(inc, SUBLANES - 1, axis=0), 1.0) * run
        a = [p_ * off for p_ in part[::-1]]
        return jnp.concatenate(a, axis=0), run * inc[0:1]

    def scores(j, z_ref, valid=None):
        j = jnp.maximum(j, 0)
        b = bias2 if valid is None else jnp.where(valid, bias2, -1e30)
        z_ref[...] = _dot(kp_ref[pl.ds(pl.multiple_of(j * bq, bq), bq), :], qt) + b

    def attend(j, z_ref, masked, runs):
        runs = list(runs)
        j = jnp.maximum(j, 0)
        for g in range(nsub - 1, -1, -1):
            tiles = []
            for lt in range(nsub):
                if masked and g > lt:
                    tiles.append(jnp.zeros((LANES, LANES), BF16))
                    continue
                rel = 0 if (masked and g == lt) else None
                tile = z_ref[g * LANES:(g + 1) * LANES, lt * LANES:(lt + 1) * LANES]
                a, runs[lt] = chain(tile, rel, runs[lt])
                tiles.append(a.astype(BF16))
            acc_ref[...] += _dot(vt_ref[j, :, g * LANES:(g + 1) * LANES], jnp.concatenate(tiles, axis=1))
        return tuple(runs)

    scores(i, z1_ref)
    runs = attend(i, z1_ref, True, tuple(jnp.ones((1, LANES), F32) for _ in range(nsub)))
    scores(i - 1, z0_ref)

    def body(t, runs):
        ja = i - 1 - 2 * t
        jb = ja - 1
        scores(jb, z1_ref, jb >= 0)
        runs = attend(ja, z0_ref, False, runs)
        scores(jb - 1, z0_ref)
        return attend(jb, z1_ref, False, runs)

    lax.fori_loop(0, (i + 1) // 2, body, runs)
    o_ref[...] = (acc_ref[...].T * _silu(gate_ref[...])).astype(o_ref.dtype)


def _sb_prompt(bias, proj, q_sec, z_sec, kp, vt, bq):
    _, t, _ = proj.shape
    assert t % bq == 0 and bq % LANES == 0
    return pl.pallas_call(
        functools.partial(_sb_body, bq=bq),
        grid=(H_B, t // bq),
        in_specs=[
            pl.BlockSpec(memory_space=pltpu.SMEM),
            pl.BlockSpec((None, bq, D_B), lambda h, i: (q_sec, i, h)),
            pl.BlockSpec((None, bq, D_B), lambda h, i: (z_sec, i, h)),
            pl.BlockSpec((None, t, D_B), lambda h, i: (h, 0, 0)),
            pl.BlockSpec((None, t // bq, D_B, bq), lambda h, i: (h, 0, 0, 0)),
        ],
        out_specs=pl.BlockSpec((bq, D_B), lambda h, i: (i, h)),
        out_shape=jax.ShapeDtypeStruct((t, H_B * D_B), BF16),
        scratch_shapes=[pltpu.VMEM((D_B, bq), F32), pltpu.VMEM((bq, bq), F32), pltpu.VMEM((bq, bq), F32)],
        compiler_params=_cparams(("parallel", "arbitrary")),
        name="sb_prompt",
    )(bias, proj, proj, kp, vt)


def _sb_key_layouts(k, v, bq):
    t = k.shape[0]
    nb = t // LANES

    def interleave(a):
        a = a.astype(BF16).reshape(nb, SUBLANES, KEY_SEG, H_B, D_B)
        return jnp.transpose(a, (3, 0, 2, 1, 4))

    kp = interleave(k).reshape(H_B, t, D_B)
    vt = jnp.swapaxes(interleave(v).reshape(H_B, t // bq, bq, D_B), 2, 3)
    return kp, vt


def _conv_rows(xp_ref, w_ref, r0, rows, lo):
    acc = None
    for i in range(CONV_W):
        start = SUBLANES - (CONV_W - 1) + i + r0
        term = xp_ref[start:start + rows, lo:lo + LANES] * w_ref[i:i + 1, lo:lo + LANES]
        acc = term if acc is None else acc + term
    return acc


def _split_bf16(a):
    hi = a.astype(BF16)
    return hi, (a - hi.astype(F32)).astype(BF16)


def _dot_split(a, b):
    ah, al = a
    bh, bl = b
    return _dot(ah, bh) + (_dot(ah, bl) + _dot(al, bh))


def _tri_inverse_unit_lower(lmats, n):
    row = lax.broadcasted_iota(jnp.int32, (n, n), 0)
    col = lax.broadcasted_iota(jnp.int32, (n, n), 1)
    eye = jnp.where(row == col, 1.0, 0.0).astype(F32)
    ms = [-l for l in lmats]
    invs = [eye + m for m in ms]
    k = 2
    while k < n:
        splits = [_split_bf16(m) for m in ms]
        ms = [_dot_split(s, s) for s in splits]
        invs = [_dot_split(_split_bf16(iv), _split_bf16(eye + m)) for iv, m in zip(invs, ms)]
        k *= 2
    return invs


def _gdn_body(q_ref, k_ref, v_ref, z_ref, ab_ref, cw_ref, alog_ref, dtb_ref, gn_ref,
              o_ref, s_out_ref, tail_ref, xp_ref, s_ref, *, cpb):
    c = GDN_CHUNK
    rows = cpb * c
    step = pl.program_id(0)

    @pl.when(step == 0)
    def _():
        xp_ref[0:SUBLANES, :] = jnp.zeros((SUBLANES, CONV_DIM_A), F32)
        s_ref[...] = jnp.zeros_like(s_ref)

    xp_ref[SUBLANES:, 0:QK_A] = q_ref[...]
    xp_ref[SUBLANES:, QK_A:2 * QK_A] = k_ref[...]
    xp_ref[SUBLANES:, 2 * QK_A:] = v_ref[...]

    ab = ab_ref[...]
    g_all = -jnp.exp(alog_ref[...]) * _softplus(ab + dtb_ref[...])
    beta_all = _sigmoid(ab)
    ri = lax.broadcasted_iota(jnp.int32, (c, c), 0)
    ci = lax.broadcasted_iota(jnp.int32, (c, c), 1)
    tril = ri >= ci
    strict = ri > ci
    tri_f = jnp.where(tril, 1.0, 0.0).astype(F32)

    items = []
    for cc in range(cpb):
        r0 = cc * c
        gc = _dot_hi(tri_f, g_all[r0:r0 + c])
        gct = gc.T
        beta_c = beta_all[r0:r0 + c]
        for h in range(H_A):
            lo = h * DK_A
            qh = _silu(_conv_rows(xp_ref, cw_ref, r0, c, lo))
            kh = _silu(_conv_rows(xp_ref, cw_ref, r0, c, QK_A + lo))
            vh = _silu(_conv_rows(xp_ref, cw_ref, r0, c, 2 * QK_A + lo))
            qh = qh * lax.rsqrt(jnp.sum(qh * qh, axis=-1, keepdims=True) + 1e-6) * (DK_A ** -0.5)
            kh = kh * lax.rsqrt(jnp.sum(kh * kh, axis=-1, keepdims=True) + 1e-6)
            gcol = gc[:, h:h + 1]
            grow = gct[h:h + 1, :]
            beta = beta_c[:, H_A + h:H_A + h + 1]
            decay = jnp.where(tril, jnp.exp(jnp.where(tril, gcol - grow, 0.0)), 0.0)
            kb = kh.astype(BF16)
            lmat = jnp.where(strict, beta * _dot_nt(kb, kb) * decay, 0.0)
            egc = jnp.exp(gcol)
            glast = gcol[c - 1:c]
            items.append(dict(
                r0=r0, h=h, lmat=lmat,
                rhs=jnp.concatenate([vh * beta, kh * (beta * egc)], axis=-1),
                qk=jnp.where(tril, _dot_nt(qh.astype(BF16), kb) * decay, 0.0).astype(BF16),
                qg=(qh * egc).astype(BF16),
                kdec_t=(kh * jnp.exp(glast - gcol)).T.astype(BF16),
                sdecay=jnp.exp(glast)))
    invs = _tri_inverse_unit_lower([it["lmat"] for it in items], c)
    sols = [_dot_split(_split_bf16(iv), _split_bf16(it["rhs"])) for iv, it in zip(invs, items)]

    for it, sol in zip(items, sols):
        r0, h = it["r0"], it["h"]
        lo = h * DV_A
        s_old = s_ref[h]
        sb = s_old.astype(BF16)
        v_new = sol[:, :DV_A] - _dot(sol[:, DV_A:].astype(BF16), sb)
        vnb = v_new.astype(BF16)
        o = _dot(it["qg"], sb) + _dot(it["qk"], vnb)
        s_ref[h] = s_old * it["sdecay"] + _dot(it["kdec_t"], vnb)
        o = _rms(o, gn_ref[...]) * _silu(z_ref[r0:r0 + c, lo:lo + DV_A])
        o_ref[r0:r0 + c, lo:lo + DV_A] = o.astype(o_ref.dtype)

    xp_ref[0:SUBLANES, :] = xp_ref[rows:rows + SUBLANES, :]

    @pl.when(step == pl.num_programs(0) - 1)
    def _():
        s_out_ref[...] = s_ref[...]
        tail_ref[...] = xp_ref[0:SUBLANES, :]


def _gdn_prompt(proj, ab, conv_w, a_log_row, dt_bias_row, g_norm, cpb):
    _, t, _ = proj.shape
    rows = cpb * GDN_CHUNK
    assert t % rows == 0
    sec = lambda s: pl.BlockSpec((None, rows, SEC), lambda i: (s, i, 0))
    full = lambda shape: pl.BlockSpec(shape, lambda i: (0,) * len(shape))
    return pl.pallas_call(
        functools.partial(_gdn_body, cpb=cpb),
        grid=(t // rows,),
        in_specs=[sec(0), sec(1), sec(2), sec(3),
                  pl.BlockSpec((rows, LANES), lambda i: (i, 0)),
                  full((CONV_W, CONV_DIM_A)), full((1, LANES)), full((1, LANES)), full((1, DV_A))],
        out_specs=[pl.BlockSpec((rows, V_A), lambda i: (i, 0)),
                   full((H_A, DK_A, DV_A)), full((SUBLANES, CONV_DIM_A))],
        out_shape=[jax.ShapeDtypeStruct((t, V_A), BF16),
                   jax.ShapeDtypeStruct((H_A, DK_A, DV_A), F32),
                   jax.ShapeDtypeStruct((SUBLANES, CONV_DIM_A), F32)],
        scratch_shapes=[pltpu.VMEM((rows + SUBLANES, CONV_DIM_A), F32),
                        pltpu.VMEM((H_A, DK_A, DV_A), F32)],
        compiler_params=_cparams(("arbitrary",)),
        name="gdn_prompt",
    )(proj, proj, proj, proj, ab, conv_w, a_log_row, dt_bias_row, g_norm)


def _ssd_body(z0_ref, z1_ref, x0_ref, x1_ref, bc_ref, dt_ref, cw_ref, cb_ref, alog_ref, dtb_ref,
              dskip_ref, gn_ref, exp_ref, o_ref, s_out_ref, tail_ref, xp_ref, s_ref, *, cpb):
    c = SSD_CHUNK
    rows = cpb * c
    step = pl.program_id(0)
    gw = D_INNER_C // N_GROUPS_C
    n_bc = N_GROUPS_C * D_STATE_C

    @pl.when(step == 0)
    def _():
        xp_ref[0:SUBLANES, :] = jnp.zeros((SUBLANES, CONV_DIM_C), F32)
        s_ref[...] = jnp.zeros_like(s_ref)

    xp_ref[SUBLANES:, 0:SEC] = x0_ref[...]
    xp_ref[SUBLANES:, SEC:2 * SEC] = x1_ref[...]
    xp_ref[SUBLANES:, 2 * SEC:] = bc_ref[...]

    dt_all = _softplus(dt_ref[...] + dtb_ref[...])
    da_all = dt_all * (-jnp.exp(alog_ref[...]))
    ri = lax.broadcasted_iota(jnp.int32, (c, c), 0)
    ci = lax.broadcasted_iota(jnp.int32, (c, c), 1)
    tril = ri >= ci
    tri_f = jnp.where(tril, 1.0, 0.0).astype(F32)
    lane = lax.broadcasted_iota(jnp.int32, (1, LANES), 1)
    left = lane < HEAD_DIM_C
    rowi = lax.broadcasted_iota(jnp.int32, (LANES, 1), 0)
    top = rowi < HEAD_DIM_C
    expand = exp_ref[...]

    for cc in range(cpb):
        r0 = cc * c
        dt = dt_all[r0:r0 + c]
        acum = _dot_hi(tri_f, da_all[r0:r0 + c])
        acum_t = acum.T
        dt_x = _select_dot(dt, expand)
        ac_x = _select_dot(acum, expand)
        for g in range(N_GROUPS_C):
            bm = _silu(_conv_rows(xp_ref, cw_ref, r0, c, D_INNER_C + g * D_STATE_C)
                       + cb_ref[:, D_INNER_C + g * D_STATE_C:D_INNER_C + (g + 1) * D_STATE_C])
            cm = _silu(_conv_rows(xp_ref, cw_ref, r0, c, D_INNER_C + n_bc + g * D_STATE_C)
                       + cb_ref[:, D_INNER_C + n_bc + g * D_STATE_C:D_INNER_C + n_bc + (g + 1) * D_STATE_C])
            bmb = bm.astype(BF16)
            cmb = cm.astype(BF16)
            cb = _dot_nt(cmb, bmb)
            ys = []
            for pr in range(gw // LANES):
                lo = g * gw + pr * LANES
                h0 = lo // HEAD_DIM_C
                xs = _silu(_conv_rows(xp_ref, cw_ref, r0, c, lo) + cb_ref[:, lo:lo + LANES])
                xdt = xs * dt_x[:, lo:lo + LANES]
                acx = ac_x[:, lo:lo + LANES]
                y = None
                for hh in range(2):
                    hd = h0 + hh
                    seg = acum[:, hd:hd + 1] - acum_t[hd:hd + 1, :]
                    lm = jnp.where(tril, jnp.exp(jnp.where(tril, seg, 0.0)), 0.0)
                    xm = jnp.where(left if hh == 0 else jnp.logical_not(left), xdt, 0.0)
                    term = _dot((cb * lm).astype(BF16), xm.astype(BF16))
                    y = term if y is None else y + term
                s_old = s_ref[lo:lo + LANES, :]
                y = y + _dot_nt(cmb, s_old.astype(BF16)) * jnp.exp(acx)
                dte = jnp.exp(acx[c - 1:c] - acx)
                st = _dot((xdt * dte).T.astype(BF16), bmb)
                al0 = acum[c - 1:c, h0:h0 + 1]
                al1 = acum[c - 1:c, h0 + 1:h0 + 2]
                s_ref[lo:lo + LANES, :] = s_old * jnp.exp(jnp.where(top, al0, al1)) + st
                y = y + xs * dskip_ref[:, lo:lo + LANES]
                zz = (z0_ref if lo < SEC else z1_ref)[r0:r0 + c, (lo % SEC):(lo % SEC) + LANES]
                ys.append(y * _silu(zz))
            yg = jnp.concatenate(ys, axis=-1)
            yg = _rms(yg, gn_ref[:, g * gw:(g + 1) * gw])
            o_ref[r0:r0 + c, g * gw:(g + 1) * gw] = yg.astype(o_ref.dtype)

    xp_ref[0:SUBLANES, :] = xp_ref[rows:rows + SUBLANES, :]

    @pl.when(step == pl.num_programs(0) - 1)
    def _():
        s_out_ref[...] = s_ref[...]
        tail_ref[...] = xp_ref[0:SUBLANES, :]


def _ssd_prompt(proj, dt_raw, conv_w, conv_b, a_log_row, dt_bias_row, dskip_x, g_norm, expand, cpb):
    _, t, _ = proj.shape
    rows = cpb * SSD_CHUNK
    assert t % rows == 0
    sec = lambda s: pl.BlockSpec((None, rows, SEC), lambda i: (s, i, 0))
    full = lambda shape: pl.BlockSpec(shape, lambda i: (0,) * len(shape))
    return pl.pallas_call(
        functools.partial(_ssd_body, cpb=cpb),
        grid=(t // rows,),
        in_specs=[sec(0), sec(1), sec(2), sec(3), sec(4),
                  pl.BlockSpec((rows, LANES), lambda i: (i, 0)),
                  full((CONV_W, CONV_DIM_C)), full((1, CONV_DIM_C)), full((1, LANES)), full((1, LANES)),
                  full((1, D_INNER_C)), full((1, D_INNER_C)), full((LANES, D_INNER_C))],
        out_specs=[pl.BlockSpec((rows, D_INNER_C), lambda i: (i, 0)),
                   full((D_INNER_C, D_STATE_C)), full((SUBLANES, CONV_DIM_C))],
        out_shape=[jax.ShapeDtypeStruct((t, D_INNER_C), BF16),
                   jax.ShapeDtypeStruct((D_INNER_C, D_STATE_C), F32),
                   jax.ShapeDtypeStruct((SUBLANES, CONV_DIM_C), F32)],
        scratch_shapes=[pltpu.VMEM((rows + SUBLANES, CONV_DIM_C), F32),
                        pltpu.VMEM((D_INNER_C, D_STATE_C), F32)],
        compiler_params=_cparams(("arbitrary",)),
        name="ssd_prompt",
    )(proj, proj, proj, proj, proj, dt_raw, conv_w, conv_b, a_log_row, dt_bias_row, dskip_x, g_norm, expand)


def _head_expand():
    r = jnp.arange(LANES)[:, None]
    cidx = jnp.arange(D_INNER_C)[None, :]
    return (cidx // HEAD_DIM_C == r).astype(BF16)


def _row_to_col(row, eye):
    return jnp.sum(jnp.where(eye, row, 0.0), axis=1, keepdims=True)


def _col_to_row(col, eye):
    return jnp.sum(jnp.where(eye, col, 0.0), axis=0, keepdims=True)


def _eye(n):
    return lax.broadcasted_iota(jnp.int32, (n, n), 0) == lax.broadcasted_iota(jnp.int32, (n, n), 1)


def _conv_step(x_row, buf_ref, w_ref, lo, hi):
    acc = x_row * w_ref[CONV_W - 1:CONV_W, lo:hi]
    for i in range(CONV_W - 1):
        acc = acc + buf_ref[i:i + 1, lo:hi] * w_ref[i:i + 1, lo:hi]
    return acc


def _gdn_step_body(q_ref, k_ref, v_ref, z_ref, ab_ref, buf_ref, s_ref, cw_ref, alog_ref, dtb_ref, gn_ref,
                   o_ref, s_out_ref, buf_out_ref):
    eye = _eye(LANES)
    ab = ab_ref[...]
    eg_all = jnp.exp(-jnp.exp(alog_ref[...]) * _softplus(ab + dtb_ref[...]))
    beta_all = _sigmoid(ab)
    buf_out_ref[0:CONV_W - 2, :] = buf_ref[1:CONV_W - 1, :]
    pre = (q_ref, k_ref, v_ref)
    for n in range(3):
        buf_out_ref[CONV_W - 2:CONV_W - 1, n * SEC:(n + 1) * SEC] = pre[n][...]
    for h in range(H_A):
        lo = h * DK_A
        q = _silu(_conv_step(q_ref[:, lo:lo + DK_A], buf_ref, cw_ref, lo, lo + DK_A))
        k = _silu(_conv_step(k_ref[:, lo:lo + DK_A], buf_ref, cw_ref, QK_A + lo, QK_A + lo + DK_A))
        v = _silu(_conv_step(v_ref[:, lo:lo + DV_A], buf_ref, cw_ref, 2 * QK_A + lo, 2 * QK_A + lo + DV_A))
        q = q * lax.rsqrt(jnp.sum(q * q, axis=-1, keepdims=True) + 1e-6) * (DK_A ** -0.5)
        k = k * lax.rsqrt(jnp.sum(k * k, axis=-1, keepdims=True) + 1e-6)
        eg = eg_all[:, h:h + 1]
        beta = beta_all[:, H_A + h:H_A + h + 1]
        s0 = s_ref[h]
        kc = _row_to_col(k, eye)
        qc = _row_to_col(q, eye)
        sk = jnp.sum(s0 * kc, axis=0, keepdims=True)
        sq = jnp.sum(s0 * qc, axis=0, keepdims=True)
        v_new = beta * (v - eg * sk)
        o = eg * sq + jnp.sum(q * k, axis=-1, keepdims=True) * v_new
        s_out_ref[h] = s0 * eg + kc * v_new
        o_ref[:, lo:lo + DV_A] = _rms(o, gn_ref[...]) * _silu(z_ref[:, lo:lo + DV_A])


def _gdn_step(proj, ab, conv_buf, state, conv_w, a_log_row, dt_bias_row, g_norm):
    b = ab.shape[0]
    sec = lambda s: pl.BlockSpec((None, None, 1, SEC), lambda i: (s, i, 0, 0))
    full = lambda shape: pl.BlockSpec(shape, lambda i: (0,) * len(shape))
    return pl.pallas_call(
        _gdn_step_body,
        grid=(b,),
        in_specs=[sec(0), sec(1), sec(2), sec(3),
                  pl.BlockSpec((None, 1, LANES), lambda i: (i, 0, 0)),
                  pl.BlockSpec((None, CONV_W - 1, CONV_DIM_A), lambda i: (i, 0, 0)),
                  pl.BlockSpec((None, H_A, DK_A, DV_A), lambda i: (i, 0, 0, 0)),
                  full((CONV_W, CONV_DIM_A)), full((1, LANES)), full((1, LANES)), full((1, DV_A))],
        out_specs=[pl.BlockSpec((None, 1, V_A), lambda i: (i, 0, 0)),
                   pl.BlockSpec((None, H_A, DK_A, DV_A), lambda i: (i, 0, 0, 0)),
                   pl.BlockSpec((None, CONV_W - 1, CONV_DIM_A), lambda i: (i, 0, 0))],
        out_shape=[jax.ShapeDtypeStruct((b, 1, V_A), F32),
                   jax.ShapeDtypeStruct((b, H_A, DK_A, DV_A), F32),
                   jax.ShapeDtypeStruct((b, CONV_W - 1, CONV_DIM_A), F32)],
        compiler_params=_cparams(("parallel",)),
        name="gdn_step",
    )(proj, proj, proj, proj, ab, conv_buf, state, conv_w, a_log_row, dt_bias_row, g_norm)


def _ssd_step_body(z0_ref, z1_ref, x0_ref, x1_ref, bc_ref, dt_ref, buf_ref, s_ref, cw_ref, cb_ref, alog_ref,
                   dtb_ref, dskip_ref, gn_ref, o_ref, s_out_ref, buf_out_ref):
    eye = _eye(LANES)
    gw = D_INNER_C // N_GROUPS_C
    n_bc = N_GROUPS_C * D_STATE_C
    top = lax.broadcasted_iota(jnp.int32, (LANES, 1), 0) < HEAD_DIM_C
    dt_all = _softplus(dt_ref[...] + dtb_ref[...])
    dec_all = jnp.exp(dt_all * (-jnp.exp(alog_ref[...])))
    buf_out_ref[0:CONV_W - 2, :] = buf_ref[1:CONV_W - 1, :]
    pre = (x0_ref, x1_ref, bc_ref)
    for n in range(3):
        buf_out_ref[CONV_W - 2:CONV_W - 1, n * SEC:(n + 1) * SEC] = pre[n][...]

    def conv(ref, lo_in, lo):
        return _silu(_conv_step(ref[:, lo_in:lo_in + LANES], buf_ref, cw_ref, lo, lo + LANES)
                     + cb_ref[:, lo:lo + LANES])

    for g in range(N_GROUPS_C):
        bm = conv(bc_ref, g * D_STATE_C, D_INNER_C + g * D_STATE_C)
        cm = conv(bc_ref, n_bc + g * D_STATE_C, D_INNER_C + n_bc + g * D_STATE_C)
        ys = []
        for pr in range(gw // LANES):
            lo = g * gw + pr * LANES
            h0 = lo // HEAD_DIM_C
            xs = conv(x0_ref if lo < SEC else x1_ref, lo % SEC, lo)
            xc = _row_to_col(xs, eye)
            dt_c = jnp.where(top, dt_all[:, h0:h0 + 1], dt_all[:, h0 + 1:h0 + 2])
            dec_c = jnp.where(top, dec_all[:, h0:h0 + 1], dec_all[:, h0 + 1:h0 + 2])
            h_new = s_ref[lo:lo + LANES, :] * dec_c + (xc * dt_c) * bm
            s_out_ref[lo:lo + LANES, :] = h_new
            y = _col_to_row(jnp.sum(h_new * cm, axis=1, keepdims=True), eye)
            y = y + xs * dskip_ref[:, lo:lo + LANES]
            zz = (z0_ref if lo < SEC else z1_ref)[:, (lo % SEC):(lo % SEC) + LANES]
            ys.append(y * _silu(zz))
        yg = jnp.concatenate(ys, axis=-1)
        o_ref[:, g * gw:(g + 1) * gw] = _rms(yg, gn_ref[:, g * gw:(g + 1) * gw])


def _ssd_step(proj, dt_raw, conv_buf, state, conv_w, conv_b, a_log_row, dt_bias_row, dskip_x, g_norm):
    b = dt_raw.shape[0]
    sec = lambda s: pl.BlockSpec((None, None, 1, SEC), lambda i: (s, i, 0, 0))
    full = lambda shape: pl.BlockSpec(shape, lambda i: (0,) * len(shape))
    return pl.pallas_call(
        _ssd_step_body,
        grid=(b,),
        in_specs=[sec(0), sec(1), sec(2), sec(3), sec(4),
                  pl.BlockSpec((None, 1, LANES), lambda i: (i, 0, 0)),
                  pl.BlockSpec((None, CONV_W - 1, CONV_DIM_C), lambda i: (i, 0, 0)),
                  pl.BlockSpec((None, D_INNER_C, D_STATE_C), lambda i: (i, 0, 0)),
                  full((CONV_W, CONV_DIM_C)), full((1, CONV_DIM_C)), full((1, LANES)), full((1, LANES)),
                  full((1, D_INNER_C)), full((1, D_INNER_C))],
        out_specs=[pl.BlockSpec((None, 1, D_INNER_C), lambda i: (i, 0, 0)),
                   pl.BlockSpec((None, D_INNER_C, D_STATE_C), lambda i: (i, 0, 0)),
                   pl.BlockSpec((None, CONV_W - 1, CONV_DIM_C), lambda i: (i, 0, 0))],
        out_shape=[jax.ShapeDtypeStruct((b, 1, D_INNER_C), F32),
                   jax.ShapeDtypeStruct((b, D_INNER_C, D_STATE_C), F32),
                   jax.ShapeDtypeStruct((b, CONV_W - 1, CONV_DIM_C), F32)],
        compiler_params=_cparams(("parallel",)),
        name="ssd_step",
    )(proj, proj, proj, proj, proj, dt_raw, conv_buf, state, conv_w, conv_b, a_log_row, dt_bias_row,
      dskip_x, g_norm)


SB_PAGES_PER_STEP = 16


def _sb_paged_body(pt_ref, q_ref, z_ref, bias_ref, *refs, ppb):
    del pt_ref
    k_refs = refs[:ppb]
    v_refs = refs[ppb:2 * ppb]
    o_ref, acc_ref, run_ref = refs[2 * ppb:]
    p = pl.program_id(1)
    grp = PAGE_SIZE // KEY_SEG
    rows = KEY_SEG * H_B

    @pl.when(p == 0)
    def _():
        acc_ref[...] = jnp.zeros_like(acc_ref)
        run_ref[...] = jnp.zeros_like(run_ref)

    sub = lax.broadcasted_iota(jnp.int32, (SUBLANES, LANES), 0)
    lane = lax.broadcasted_iota(jnp.int32, (SUBLANES, LANES), 1)
    own = (lane % H_B) == sub
    ri = lax.broadcasted_iota(jnp.int32, (rows, 2 * rows), 0)
    ci = lax.broadcasted_iota(jnp.int32, (rows, 2 * rows), 1)
    same_head = (ri % H_B) == (ci % H_B)
    tmat = jnp.where(same_head & ((ci >= rows) | (ri // H_B >= ci // H_B)), 1.0, 0.0).astype(BF16)
    qb = q_ref[...].astype(BF16)
    bias = bias_ref[...]
    pages = range(ppb)
    k2 = [k_refs[r][...].reshape(PAGE_SIZE * H_B, D_B).astype(BF16) for r in pages]
    zs = []
    for r in pages:
        zrows = []
        for g in range(grp):
            zt = _dot_nt(qb, k2[r][g * rows:(g + 1) * rows])
            zrows.append(jnp.sum(jnp.where(own, zt, 0.0), axis=0, keepdims=True))
        zs.append(jnp.concatenate(zrows, axis=0) * (D_B ** -0.5) + bias)
    sps = [_split_bf16(_softplus(z)) for z in zs]
    css = [_dot(hi, tmat) + _dot(lo, tmat) for hi, lo in sps]
    offs = []
    tots = []
    for cs in css:
        inc = cs[:, rows:]
        for d in (1, 2, 4):
            sh = pltpu.roll(inc, SUBLANES - d, axis=0)
            inc = inc + jnp.where(sub + d < SUBLANES, sh, 0.0)
        offs.append(jnp.where(sub < SUBLANES - 1, pltpu.roll(inc, SUBLANES - 1, axis=0), 0.0))
        tots.append(inc[0:1])
    run_after = run_ref[...]
    acc = acc_ref[...]
    for r in pages:
        a = jnp.exp(zs[r] - (css[r][:, :rows] + (offs[r] + run_after)))
        lhs = jnp.concatenate([jnp.where(own, a[g:g + 1], 0.0) for g in range(grp)], axis=1)
        v2 = v_refs[r][...].reshape(PAGE_SIZE * H_B, D_B).astype(BF16)
        acc = acc + _dot(lhs.astype(BF16), v2)
        run_after = run_after + tots[r]
    acc_ref[...] = acc
    run_ref[...] = run_after

    @pl.when(p == pl.num_programs(1) - 1)
    def _():
        o_ref[...] = acc * _silu(z_ref[...])


def _sb_paged(page_table, q, zgate, bias_row, cache_k, cache_v, layer):
    b, npg = page_table.shape
    ppb = SB_PAGES_PER_STEP
    assert npg % ppb == 0

    def page_spec(r):
        return pl.BlockSpec((None, None, PAGE_SIZE, H_B, D_B),
                            lambda i, p, pt: (layer, pt[i, npg - 1 - (p * ppb + r)], 0, 0, 0))

    row = pl.BlockSpec((None, H_B, D_B), lambda i, p, pt: (i, 0, 0))
    grid_spec = pltpu.PrefetchScalarGridSpec(
        num_scalar_prefetch=1,
        grid=(b, npg // ppb),
        in_specs=[row, row, pl.BlockSpec((1, LANES), lambda i, p, pt: (0, 0))]
        + [page_spec(r) for r in range(ppb)] * 2,
        out_specs=row,
        scratch_shapes=[pltpu.VMEM((H_B, D_B), F32), pltpu.VMEM((1, LANES), F32)],
    )
    return pl.pallas_call(
        functools.partial(_sb_paged_body, ppb=ppb),
        grid_spec=grid_spec,
        out_shape=jax.ShapeDtypeStruct((b, H_B, D_B), F32),
        compiler_params=_cparams(("parallel", "arbitrary")),
        name="sb_paged",
    )(page_table, q, zgate, bias_row, *([cache_k] * ppb), *([cache_v] * ppb))


def _pad_lanes(v):
    return jnp.pad(v.astype(F32), (0, LANES - v.shape[0])).reshape(1, LANES)


def _even_weights(w_in):
    o_ab = CONV_DIM_A
    o_z = o_ab + 2 * H_A
    w_main = jnp.concatenate([w_in[:, :o_ab], w_in[:, o_z:]], axis=1).astype(BF16)
    w_sec = jnp.transpose(w_main.reshape(D_MODEL, 8, SEC), (1, 0, 2))
    w_small = jnp.pad(w_in[:, o_ab:o_z], ((0, 0), (0, LANES - 2 * H_A))).astype(BF16)
    return w_sec, w_small


def _odd_weights(w_in):
    n_main = D_INNER_C + CONV_DIM_C
    w_sec = jnp.transpose(w_in[:, :n_main].astype(BF16).reshape(D_MODEL, 5, SEC), (1, 0, 2))
    w_small = jnp.pad(w_in[:, n_main:], ((0, 0), (0, LANES - H_C))).astype(BF16)
    return w_sec, w_small


def kernel(x_prompt, x_sample, cache_sb_k, cache_sb_v, state_gdn, state_gdn_conv, state_ssd, state_ssd_conv, page_table, even_norm_pre, even_norm_post, even_w_in, gdn_conv_w, gdn_a_log, gdn_dt_bias, gdn_norm_w, sb_bias, even_w_out, odd_norm_pre, odd_norm_post, odd_w_in, ssd_conv_w, ssd_conv_b, ssd_a_log, ssd_dt_bias, ssd_d, ssd_norm_w, odd_w_out):
    t = x_prompt.shape[1]
    bs = x_sample.shape[0]
    xp = x_prompt[0]
    xs = x_sample[:, 0]
    tail = SUBLANES - (CONV_W - 1)
    tm_p = 1024
    tm_s = bs

    w_sec, w_small = _even_weights(even_w_in[0])
    w_out_e = even_w_out[0].astype(BF16)
    a_log_e = _pad_lanes(gdn_a_log[0])
    dt_bias_e = _pad_lanes(gdn_dt_bias[0])
    proj, ab, k_b, v_b = _proj(xp, even_norm_pre, w_sec, w_small, tm_p, own=(5, 6))
    sb_k_prompt = k_b.reshape(1, 1, t, H_B, D_B)
    sb_v_prompt = v_b.reshape(1, 1, t, H_B, D_B)
    kp, vt = _sb_key_layouts(k_b, v_b, SB_BQ)
    o_b = _sb_prompt(sb_bias[0], proj, 4, 5, kp, vt, SB_BQ)
    o_a, gdn_s, gdn_tail = _gdn_prompt(proj, ab, gdn_conv_w[0], a_log_e, dt_bias_e, gdn_norm_w, 4)
    y1 = _outproj(o_a, o_b, w_out_e, even_norm_post, xp, tm_p)
    proj_s, ab_s, k_b_s, v_b_s = _proj(xs, even_norm_pre, w_sec, w_small, tm_s, own=(5, 6))
    sb_k_sample = k_b_s.reshape(1, bs, 1, H_B, D_B)
    sb_v_sample = v_b_s.reshape(1, bs, 1, H_B, D_B)
    o_b_s = _sb_paged(page_table, proj_s[4].reshape(bs, H_B, D_B), proj_s[5].reshape(bs, H_B, D_B),
                      jnp.tile(sb_bias[0].astype(F32), LANES // H_B).reshape(1, LANES),
                      cache_sb_k, cache_sb_v, 0)
    o_a_s, gdn_s_s, gdn_conv_s = _gdn_step(proj_s.reshape(6, bs, 1, SEC), ab_s.reshape(bs, 1, LANES),
                                           state_gdn_conv[0], state_gdn[0], gdn_conv_w[0], a_log_e,
                                           dt_bias_e, gdn_norm_w)
    y1_s = _outproj(o_a_s.reshape(bs, V_A), o_b_s.reshape(bs, W_B), w_out_e, even_norm_post, xs, tm_s)

    w_sec_o, w_small_o = _odd_weights(odd_w_in[0])
    w_out_o = odd_w_out[0].astype(BF16)
    a_log_o = _pad_lanes(ssd_a_log[0])
    dt_bias_o = _pad_lanes(ssd_dt_bias[0])
    dskip_x = jnp.repeat(ssd_d[0].astype(F32), HEAD_DIM_C).reshape(1, D_INNER_C)
    proj_o, dt_raw = _proj(y1, odd_norm_pre, w_sec_o, w_small_o, tm_p)
    y_ssd, ssd_s, ssd_tail = _ssd_prompt(proj_o, dt_raw, ssd_conv_w[0], ssd_conv_b, a_log_o, dt_bias_o,
                                         dskip_x, ssd_norm_w, _head_expand(), 2)
    y2 = _outproj(y_ssd, y_ssd, w_out_o, odd_norm_post, y1, tm_p, 0, 1)
    proj_os, dt_raw_s = _proj(y1_s, odd_norm_pre, w_sec_o, w_small_o, tm_s)
    y_ssd_s, ssd_s_s, ssd_conv_s = _ssd_step(proj_os.reshape(5, bs, 1, SEC), dt_raw_s.reshape(bs, 1, LANES),
                                             state_ssd_conv[0], state_ssd[0].reshape(bs, D_INNER_C, D_STATE_C),
                                             ssd_conv_w[0], ssd_conv_b, a_log_o, dt_bias_o, dskip_x, ssd_norm_w)
    y_ssd_s = y_ssd_s.reshape(bs, D_INNER_C)
    y2_s = _outproj(y_ssd_s, y_ssd_s, w_out_o, odd_norm_post, y1_s, tm_s, 0, 1)

    return (y2[None], y2_s[:, None], sb_k_prompt, sb_v_prompt, sb_k_sample, sb_v_sample,
            gdn_s[None, None], gdn_s_s[None], gdn_tail[tail:][None, None], gdn_conv_s[None],
            ssd_s.reshape(1, 1, H_C, HEAD_DIM_C, D_STATE_C),
            ssd_s_s.reshape(1, bs, H_C, HEAD_DIM_C, D_STATE_C),
            ssd_tail[tail:][None, None], ssd_conv_s[None])
```
